```python
import math
import jax, jax.numpy as jnp
from jax import lax
import numpy as np

D_MODEL = 2048
BATCH = 8
SEQ = 2048
DEPTH = 2

GRID_W = 64
CTX_LEN = 256
HEAD_DIM = 128
NA_HEADS = 6
NA_WIN_ROWS = 8
NA_WIN_COLS = 16
GQ_HEADS = 6
GQ_KV_HEADS = 2
GQ_GROUP = GQ_HEADS // GQ_KV_HEADS
GQ_BLOCK = 128
ROPE_THETA = 10000.0
SSM_GROUP = 16
SSM_WIDTH = 512
SSM_GROUPS = SSM_WIDTH // SSM_GROUP
SSM_STATE = 64
SSM_DT_MIN = 1e-3
SSM_DT_MAX = 1e-1
N_BRANCH = 3
NA_WIDTH = NA_HEADS * HEAD_DIM
GQ_WIDTH = GQ_HEADS * HEAD_DIM
GQ_KV_WIDTH = GQ_KV_HEADS * HEAD_DIM
CTX_IN_WIDTHS = (NA_WIDTH, NA_WIDTH, GQ_KV_WIDTH, GQ_KV_WIDTH, SSM_WIDTH)
IN_WIDTHS = CTX_IN_WIDTHS + (NA_WIDTH, GQ_WIDTH, N_BRANCH * D_MODEL)
CTX_IN_WIDTH = sum(CTX_IN_WIDTHS)
IN_WIDTH = sum(IN_WIDTHS)
N_EXPERTS = 64
TOP_K = 8
EXPERT_DIM = 512
SHARED_DIM = 512
ROUTED_SCALE = 2.5
MOE_BLOCK = 128
DN_ALPHA = (2 * DEPTH) ** 0.25
DN_BETA = (8 * DEPTH) ** -0.25
LN_EPS = 1e-6
RMS_EPS = 1e-6
F32 = jnp.float32

kernel_name = 'hybrid_na_gqa_s5_moe_dit'


def _split(z, widths):
    return jnp.split(z, [int(i) for i in np.cumsum(widths)[:-1]], axis=-1)


def layer_norm(x, g, b):
    xf = x.astype(F32)
    mu = jnp.mean(xf, axis=-1, keepdims=True)
    var = jnp.mean(jnp.square(xf - mu), axis=-1, keepdims=True)
    return ((xf - mu) * lax.rsqrt(var + LN_EPS) * g.astype(F32) + b.astype(F32)).astype(x.dtype)


def rms_norm(x, g):
    xf = x.astype(F32)
    return (xf * lax.rsqrt(jnp.mean(xf * xf, axis=-1, keepdims=True) + RMS_EPS) * g.astype(F32)).astype(x.dtype)


def axial_rope(n_tokens):
    t = jnp.arange(n_tokens, dtype=jnp.int32)
    row = (t // GRID_W).astype(F32)
    col = (t % GRID_W).astype(F32)
    axis_dim = HEAD_DIM // 2
    inv_freq = 1.0 / (ROPE_THETA ** (jnp.arange(0, axis_dim, 2, dtype=F32) / axis_dim))
    ang = jnp.concatenate([row[:, None] * inv_freq, col[:, None] * inv_freq], axis=-1)
    return jnp.cos(ang), jnp.sin(ang)


def apply_rope(x, cos, sin):
    xf = x.astype(F32).reshape(x.shape[:-1] + (HEAD_DIM // 2, 2))
    x1, x2 = xf[..., 0], xf[..., 1]
    c = cos[None, :, None, :]
    s = sin[None, :, None, :]
    out = jnp.stack([x1 * c - x2 * s, x1 * s + x2 * c], axis=-1)
    return out.reshape(x.shape).astype(x.dtype)


def attend(q, k, v):
    s = jnp.einsum('bqhgd,bkhd->bhgqk', q, k, preferred_element_type=F32) * (HEAD_DIM ** -0.5)
    p = jax.nn.softmax(s, axis=-1).astype(v.dtype)
    return jnp.einsum('bhgqk,bkhd->bqhgd', p, v)


def neighbourhood_attention(q, k, v, k_ctx, v_ctx, rpb, rows):
    B, S, H, hd = q.shape
    kr = min(NA_WIN_ROWS, rows)
    qg = q.reshape(B, rows, GRID_W, H, hd)
    kg = k.reshape(B, rows, GRID_W, H, hd)
    vg = v.reshape(B, rows, GRID_W, H, hd)
    row_start = jnp.clip(jnp.arange(rows, dtype=jnp.int32) - kr // 2, 0, rows - kr)
    col = jnp.arange(GRID_W, dtype=jnp.int32)
    col_start = jnp.clip(col - NA_WIN_COLS // 2, 0, GRID_W - NA_WIN_COLS)
    col_mask = (col[None, :] >= col_start[:, None]) & (col[None, :] < col_start[:, None] + NA_WIN_COLS)
    col_idx = jnp.clip(col[None, :] - col[:, None], -(NA_WIN_COLS - 1), NA_WIN_COLS - 1) + NA_WIN_COLS - 1
    scale = hd ** -0.5

    def one_row(r):
        rs = row_start[r]
        kb = lax.dynamic_slice_in_dim(kg, rs, kr, axis=1)
        vb = lax.dynamic_slice_in_dim(vg, rs, kr, axis=1)
        qr = lax.dynamic_index_in_dim(qg, r, axis=1, keepdims=False)
        s_lat = jnp.einsum('bqhd,brkhd->bhqrk', qr, kb, preferred_element_type=F32) * scale
        row_idx = rs + jnp.arange(kr, dtype=jnp.int32) - r + NA_WIN_ROWS - 1
        bias = rpb[:, row_idx[None, :, None], col_idx[:, None, :]].astype(F32)
        s_lat = jnp.where(col_mask[:, None, :], s_lat + bias, -jnp.inf)
        s_ctx = jnp.einsum('bqhd,blhd->bhql', qr, k_ctx, preferred_element_type=F32) * scale
        s = jnp.concatenate([s_lat.reshape(B, H, GRID_W, kr * GRID_W), s_ctx], axis=-1)
        p = jax.nn.softmax(s, axis=-1).astype(v.dtype)
        p_lat, p_ctx = p[..., :kr * GRID_W], p[..., kr * GRID_W:]
        return (jnp.einsum('bhqn,bnhd->bqhd', p_lat, vb.reshape(B, kr * GRID_W, H, hd))
                + jnp.einsum('bhql,blhd->bqhd', p_ctx, v_ctx))

    out = lax.map(one_row, jnp.arange(rows, dtype=jnp.int32))
    return out.transpose(1, 0, 2, 3, 4).reshape(B, S, H, hd)


def grouped_query_attention(q, k, v, k_ctx, v_ctx):
    B, S = q.shape[:2]
    k_all = jnp.concatenate([k, k_ctx], axis=1)
    v_all = jnp.concatenate([v, v_ctx], axis=1)
    qb = q.reshape(B, S // GQ_BLOCK, GQ_BLOCK, GQ_KV_HEADS, GQ_GROUP, HEAD_DIM).transpose(1, 0, 2, 3, 4, 5)
    out = lax.map(lambda q_blk: attend(q_blk, k_all, v_all), qb)
    return out.transpose(1, 0, 2, 3, 4, 5).reshape(B, S, GQ_HEADS, HEAD_DIM)


def _cmul(ar, ai, br, bi):
    return ar * br - ai * bi, ar * bi + ai * br


def zoh_discretise(lam_r, lam_i, log_dt, b_r, b_i):
    dt = jnp.exp(log_dt)[:, None]
    mag = jnp.exp(lam_r * dt)
    ar, ai = mag * jnp.cos(lam_i * dt), mag * jnp.sin(lam_i * dt)
    den = lam_r * lam_r + lam_i * lam_i
    nr, ni = ar - 1.0, ai
    kr_, ki_ = (nr * lam_r + ni * lam_i) / den, (ni * lam_r - nr * lam_i) / den
    bbr, bbi = _cmul(kr_[..., None], ki_[..., None], b_r, b_i)
    return ar, ai, bbr, bbi


def _scan_combine(e1, e2):
    a1r, a1i, b1r, b1i = e1
    a2r, a2i, b2r, b2i = e2
    ar, ai = _cmul(a2r, a2i, a1r, a1i)
    br, bi = _cmul(a2r, a2i, b1r, b1i)
    return ar, ai, br + b2r, bi + b2i


def linear_recurrence(a_r, a_i, b_r, b_i, h0_r=None, h0_i=None):
    n = b_r.shape[0]
    a_r = jnp.broadcast_to(a_r[None, None], (n, 1) + a_r.shape)
    a_i = jnp.broadcast_to(a_i[None, None], (n, 1) + a_i.shape)
    acr, aci, hr, hi = lax.associative_scan(_scan_combine, (a_r, a_i, b_r, b_i), axis=0)
    if h0_r is not None:
        hr, hi = hr + acr * h0_r - aci * h0_i, hi + acr * h0_i + aci * h0_r
    return hr, hi


def _readout(h_r, h_i, c_r, c_i):
    return jnp.einsum('sbgp,ghp->bsgh', h_r, c_r) - jnp.einsum('sbgp,ghp->bsgh', h_i, c_i)


def s5_branch(u, uc, need_ctx_out, lam_r, lam_i, log_dt, b_r, b_i, c_r, c_i, d, glu_w, glu_b):
    B, S, _ = u.shape
    L = uc.shape[1]
    uf = u.astype(F32).reshape(B, S, SSM_GROUPS, SSM_GROUP)
    ucf = uc.astype(F32).reshape(B, L, SSM_GROUPS, SSM_GROUP)
    dg = d.astype(F32).reshape(SSM_GROUPS, SSM_GROUP)
    y = uf * dg
    yc = ucf * dg
    for direction in range(2):
        ar, ai, bbr, bbi = zoh_discretise(lam_r[direction].astype(F32), lam_i[direction].astype(F32),
                                          log_dt[direction].astype(F32), b_r[direction].astype(F32),
                                          b_i[direction].astype(F32))
        bu_r = jnp.einsum('bsgh,gph->sbgp', uf, bbr)
        bu_i = jnp.einsum('bsgh,gph->sbgp', uf, bbi)
        bc_r = jnp.einsum('bsgh,gph->sbgp', ucf, bbr)
        bc_i = jnp.einsum('bsgh,gph->sbgp', ucf, bbi)
        if direction == 1:
            bu_r, bu_i, bc_r, bc_i = (jnp.flip(t, 0) for t in (bu_r, bu_i, bc_r, bc_i))
        hc_r, hc_i = linear_recurrence(ar, ai, bc_r, bc_i)
        h_r, h_i = linear_recurrence(ar, ai, bu_r, bu_i, hc_r[-1], hc_i[-1])
        if direction == 1:
            h_r, h_i = jnp.flip(h_r, 0), jnp.flip(h_i, 0)
            hc_r, hc_i = jnp.flip(hc_r, 0), jnp.flip(hc_i, 0)
        cr, ci = c_r[direction].astype(F32), c_i[direction].astype(F32)
        y = y + _readout(h_r, h_i, cr, ci)
        if need_ctx_out:
            yc = yc + _readout(hc_r, hc_i, cr, ci)

    def glu(t):
        t = jax.nn.gelu(t.reshape(t.shape[:2] + (SSM_WIDTH,)))
        a, g = jnp.split(t @ glu_w.astype(F32) + glu_b.astype(F32), 2, axis=-1)
        return (a * jax.nn.sigmoid(g)).astype(u.dtype)

    return glu(y), (glu(yc) if need_ctx_out else None)


def token_mixer(u, uc, cos, sin, rows, need_ctx_out, w_in, na_rpb, q_norm, k_norm, ssm_params,
                w_br_na, w_br_gq, w_br_ssm, w_out):
    B, S, _ = u.shape
    L = uc.shape[1]

    def heads(t, n):
        return t.reshape(t.shape[:2] + (n, HEAD_DIM))

    na_k, na_v, gq_k, gq_v, s_u, na_q, gq_q, gates = _split(u @ w_in, IN_WIDTHS)
    if need_ctx_out:
        c_na_k, c_na_v, c_gq_k, c_gq_v, c_s_u, c_na_q, c_gq_q, c_gates = _split(uc @ w_in, IN_WIDTHS)
    else:
        c_na_k, c_na_v, c_gq_k, c_gq_v, c_s_u = _split(uc @ w_in[:, :CTX_IN_WIDTH], CTX_IN_WIDTHS)

    nk_c, nv_c = heads(c_na_k, NA_HEADS), heads(c_na_v, NA_HEADS)
    a_out = neighbourhood_attention(heads(na_q, NA_HEADS), heads(na_k, NA_HEADS), heads(na_v, NA_HEADS),
                                    nk_c, nv_c, na_rpb, rows)
    gk_c = rms_norm(heads(c_gq_k, GQ_KV_HEADS), k_norm)
    gv_c = heads(c_gq_v, GQ_KV_HEADS)
    gq = apply_rope(rms_norm(heads(gq_q, GQ_HEADS), q_norm), cos, sin)
    gk = apply_rope(rms_norm(heads(gq_k, GQ_KV_HEADS), k_norm), cos, sin)
    b_out = grouped_query_attention(gq, gk, heads(gq_v, GQ_KV_HEADS), gk_c, gv_c)
    c_out, c_out_ctx = s5_branch(s_u, c_s_u, need_ctx_out, *ssm_params)

    def merge(g, a, b, s):
        g_a, g_b, g_s = jnp.split(g, N_BRANCH, axis=-1)
        m = (jax.nn.sigmoid(g_a) * (a @ w_br_na) + jax.nn.sigmoid(g_b) * (b @ w_br_gq)
             + jax.nn.sigmoid(g_s) * (s @ w_br_ssm))
        return m @ w_out

    out = merge(gates, a_out.reshape(B, S, NA_WIDTH), b_out.reshape(B, S, GQ_WIDTH), c_out)
    if not need_ctx_out:
        return out, None
    a_ctx = attend(heads(c_na_q, NA_HEADS)[:, :, :, None, :], nk_c, nv_c)
    b_ctx = attend(rms_norm(heads(c_gq_q, GQ_HEADS), q_norm).reshape(B, L, GQ_KV_HEADS, GQ_GROUP, HEAD_DIM),
                   gk_c, gv_c)
    out_ctx = merge(c_gates, a_ctx.reshape(B, L, NA_WIDTH), b_ctx.reshape(B, L, GQ_WIDTH), c_out_ctx)
    return out, out_ctx


def swiglu(t, w1, w3, w2):
    return (jax.nn.silu(t @ w1) * (t @ w3)) @ w2


def grouped_experts(t, idx, wts, w1, w3, w2):
    T, D = t.shape
    A = T * TOP_K
    e_flat = idx.reshape(A)
    tok_flat = jnp.arange(A, dtype=jnp.int32) // TOP_K
    w_flat = wts.reshape(A)
    order = jnp.argsort(e_flat)
    e_s, tok_s, w_s = e_flat[order], tok_flat[order], w_flat[order]
    counts = jnp.bincount(e_flat, length=N_EXPERTS)
    padded = (counts + MOE_BLOCK - 1) // MOE_BLOCK * MOE_BLOCK
    pend = jnp.cumsum(padded)
    pstart = pend - padded
    ostart = jnp.cumsum(counts) - counts
    dest = pstart[e_s] + jnp.arange(A, dtype=jnp.int32) - ostart[e_s]
    n_blk = -(-(A + N_EXPERTS * (MOE_BLOCK - 1)) // MOE_BLOCK)
    cap = n_blk * MOE_BLOCK
    slot_tok = jnp.full((cap,), T, jnp.int32).at[dest].set(tok_s)
    slot_w = jnp.zeros((cap,), F32).at[dest].set(w_s)
    blk_exp = jnp.minimum(jnp.searchsorted(pend, jnp.arange(n_blk, dtype=jnp.int32) * MOE_BLOCK, side='right'),
                          N_EXPERTS - 1)
    t_pad = jnp.concatenate([t, jnp.zeros((1, D), t.dtype)], axis=0)

    def step(y, xs):
        e, tok, w = xs
        out = swiglu(t_pad[tok], w1[e], w3[e], w2[e]).astype(F32) * w[:, None]
        return y.at[tok].add(out), None

    y, _ = lax.scan(step, jnp.zeros((T + 1, D), F32),
                    (blk_exp, slot_tok.reshape(n_blk, MOE_BLOCK), slot_w.reshape(n_blk, MOE_BLOCK)))
    return y[:T].astype(t.dtype)


def moe_ffn(t, router_w, router_bias, w1, w3, w2, sw1, sw3, sw2):
    scores = jax.nn.sigmoid(jnp.matmul(t, router_w, preferred_element_type=F32))
    _, idx = lax.top_k(scores + router_bias.astype(F32), TOP_K)
    wts = jnp.take_along_axis(scores, idx, axis=-1)
    wts = wts / jnp.sum(wts, axis=-1, keepdims=True) * ROUTED_SCALE
    return swiglu(t, sw1, sw3, sw2) + grouped_experts(t, idx, wts, w1, w3, w2)


def setup_inputs(seed: int = 0) -> dict:
    key = jax.random.key(seed)
    keys = iter(jax.random.split(key, 40))

    def nrm(shape, scale):
        return jax.random.normal(next(keys), shape, F32) * scale

    D = D_MODEL
    G, P, H = SSM_GROUPS, SSM_STATE, SSM_GROUP
    x = nrm((BATCH, SEQ, D), 1.0)
    c = nrm((BATCH, D), 1.0)
    ctx = nrm((BATCH, CTX_LEN, D), 1.0)
    c_ctx = nrm((D,), 1.0)
    w_mod = nrm((DEPTH, D, 6 * D), D ** -0.5)
    b_mod = nrm((DEPTH, 6 * D), 0.02)
    w_in = nrm((DEPTH, D, IN_WIDTH), D ** -0.5)
    na_rpb = nrm((DEPTH, NA_HEADS, 2 * NA_WIN_ROWS - 1, 2 * NA_WIN_COLS - 1), 0.02)
    gq_q_norm = 1.0 + nrm((DEPTH, HEAD_DIM), 0.02)
    gq_k_norm = 1.0 + nrm((DEPTH, HEAD_DIM), 0.02)
    ssm_lam_re = -0.5 + nrm((DEPTH, 2, G, P), 0.01)
    ssm_lam_im = math.pi * jnp.arange(P, dtype=F32) + nrm((DEPTH, 2, G, P), 0.01)
    ssm_log_dt = jax.random.uniform(next(keys), (DEPTH, 2, G), F32,
                                    minval=math.log(SSM_DT_MIN), maxval=math.log(SSM_DT_MAX))
    ssm_b_re = nrm((DEPTH, 2, G, P, H), (2 * H) ** -0.5)
    ssm_b_im = nrm((DEPTH, 2, G, P, H), (2 * H) ** -0.5)
    ssm_c_re = nrm((DEPTH, 2, G, H, P), (2 * P) ** -0.5)
    ssm_c_im = nrm((DEPTH, 2, G, H, P), (2 * P) ** -0.5)
    ssm_d = nrm((DEPTH, SSM_WIDTH), 1.0)
    ssm_glu_w = nrm((DEPTH, SSM_WIDTH, 2 * SSM_WIDTH), SSM_WIDTH ** -0.5)
    ssm_glu_b = nrm((DEPTH, 2 * SSM_WIDTH), 0.02)
    w_br_na = nrm((DEPTH, NA_WIDTH, D), NA_WIDTH ** -0.5)
    w_br_gq = nrm((DEPTH, GQ_WIDTH, D), GQ_WIDTH ** -0.5)
    w_br_ssm = nrm((DEPTH, SSM_WIDTH, D), SSM_WIDTH ** -0.5)
    w_out = nrm((DEPTH, D, D), D ** -0.5 * DN_BETA)
    ln1_g = 1.0 + nrm((DEPTH, D), 0.02)
    ln1_b = nrm((DEPTH, D), 0.02)
    router_w = nrm((DEPTH, D, N_EXPERTS), D ** -0.5)
    router_bias = nrm((DEPTH, N_EXPERTS), 0.01)
    exp_w1 = nrm((DEPTH, N_EXPERTS, D, EXPERT_DIM), D ** -0.5)
    exp_w3 = nrm((DEPTH, N_EXPERTS, D, EXPERT_DIM), D ** -0.5)
    exp_w2 = nrm((DEPTH, N_EXPERTS, EXPERT_DIM, D), EXPERT_DIM ** -0.5 * DN_BETA)
    sh_w1 = nrm((DEPTH, D, SHARED_DIM), D ** -0.5)
    sh_w3 = nrm((DEPTH, D, SHARED_DIM), D ** -0.5)
    sh_w2 = nrm((DEPTH, SHARED_DIM, D), SHARED_DIM ** -0.5 * DN_BETA)
    ln2_g = 1.0 + nrm((DEPTH, D), 0.02)
    ln2_b = nrm((DEPTH, D), 0.02)
    return {'x': x, 'c': c, 'ctx': ctx, 'c_ctx': c_ctx, 'w_mod': w_mod, 'b_mod': b_mod, 'w_in': w_in,
            'na_rpb': na_rpb, 'gq_q_norm': gq_q_norm, 'gq_k_norm': gq_k_norm,
            'ssm_lam_re': ssm_lam_re, 'ssm_lam_im': ssm_lam_im, 'ssm_log_dt': ssm_log_dt,
            'ssm_b_re': ssm_b_re, 'ssm_b_im': ssm_b_im, 'ssm_c_re': ssm_c_re, 'ssm_c_im': ssm_c_im,
            'ssm_d': ssm_d, 'ssm_glu_w': ssm_glu_w, 'ssm_glu_b': ssm_glu_b,
            'w_br_na': w_br_na, 'w_br_gq': w_br_gq, 'w_br_ssm': w_br_ssm, 'w_out': w_out,
            'ln1_g': ln1_g, 'ln1_b': ln1_b, 'router_w': router_w, 'router_bias': router_bias,
            'exp_w1': exp_w1, 'exp_w3': exp_w3, 'exp_w2': exp_w2,
            'sh_w1': sh_w1, 'sh_w3': sh_w3, 'sh_w2': sh_w2, 'ln2_g': ln2_g, 'ln2_b': ln2_b}


def reference(x, c, ctx, c_ctx, w_mod, b_mod, w_in, na_rpb, gq_q_norm, gq_k_norm,
              ssm_lam_re, ssm_lam_im, ssm_log_dt, ssm_b_re, ssm_b_im, ssm_c_re, ssm_c_im,
              ssm_d, ssm_glu_w, ssm_glu_b, w_br_na, w_br_gq, w_br_ssm, w_out, ln1_g, ln1_b,
              router_w, router_bias, exp_w1, exp_w3, exp_w2, sh_w1, sh_w3, sh_w2, ln2_g, ln2_b):
    B, S, D = x.shape
    L = ctx.shape[1]
    rows = S // GRID_W
    cos, sin = axial_rope(S)
    c_act = jax.nn.silu(c)
    cc_act = jax.nn.silu(c_ctx)
    h, hc = x, ctx
    for l in range(DEPTH):
        need_ctx = l < DEPTH - 1
        sh1, sc1, g1, sh2, sc2, g2 = [m[:, None, :] for m in jnp.split(c_act @ w_mod[l] + b_mod[l], 6, axis=-1)]
        n_mod = 6 if need_ctx else 2
        mod_c = jnp.split(cc_act @ w_mod[l][:, :n_mod * D] + b_mod[l][:n_mod * D], n_mod, axis=-1)
        u = h * (1.0 + sc1) + sh1
        uc = hc * (1.0 + mod_c[1]) + mod_c[0]
        ssm_params = (ssm_lam_re[l], ssm_lam_im[l], ssm_log_dt[l], ssm_b_re[l], ssm_b_im[l],
                      ssm_c_re[l], ssm_c_im[l], ssm_d[l], ssm_glu_w[l], ssm_glu_b[l])
        mix, mix_c = token_mixer(u, uc, cos, sin, rows, need_ctx, w_in[l], na_rpb[l], gq_q_norm[l],
                                 gq_k_norm[l], ssm_params, w_br_na[l], w_br_gq[l], w_br_ssm[l], w_out[l])
        h = layer_norm(DN_ALPHA * h + g1 * mix, ln1_g[l], ln1_b[l])
        u2 = h * (1.0 + sc2) + sh2
        if need_ctx:
            hc = layer_norm(DN_ALPHA * hc + mod_c[2] * mix_c, ln1_g[l], ln1_b[l])
            uc2 = hc * (1.0 + mod_c[4]) + mod_c[3]
            tokens = jnp.concatenate([u2.reshape(-1, D), uc2.reshape(-1, D)], axis=0)
        else:
            tokens = u2.reshape(-1, D)
        f = moe_ffn(tokens, router_w[l], router_bias[l], exp_w1[l], exp_w3[l], exp_w2[l],
                    sh_w1[l], sh_w3[l], sh_w2[l])
        h = layer_norm(DN_ALPHA * h + g2 * f[:B * S].reshape(B, S, D), ln2_g[l], ln2_b[l])
        if need_ctx:
            hc = layer_norm(DN_ALPHA * hc + mod_c[5] * f[B * S:].reshape(B, L, D), ln2_g[l], ln2_b[l])
    return h
```

```python
import functools
import math

import jax
import jax.numpy as jnp
import numpy as np
from jax import lax
from jax.experimental import pallas as pl
from jax.experimental.pallas import tpu as pltpu

F32 = jnp.float32
BF16 = jnp.bfloat16

DEPTH = 2
GRID_W = 64
HEAD_DIM = 128
NA_HEADS = 6
NA_WIN_ROWS = 8
NA_WIN_COLS = 16
GQ_HEADS = 6
GQ_KV_HEADS = 2
GQ_GROUP = GQ_HEADS // GQ_KV_HEADS
ROPE_THETA = 10000.0
SSM_GROUP = 16
SSM_WIDTH = 512
SSM_GROUPS = SSM_WIDTH // SSM_GROUP
SSM_STATE = 64
SSM_LANES = SSM_GROUPS * SSM_STATE
N_EXPERTS = 64
TOP_K = 8
EXPERT_DIM = 512
ROUTED_SCALE = 2.5
DN_ALPHA = (2 * DEPTH) ** 0.25
LN_EPS = 1e-6
RMS_EPS = 1e-6
ATT_SCALE = HEAD_DIM ** -0.5
NEG = -1e30

LANES = 128
SUBLANES = 8
VMEM_LIMIT = 56 * 1024 * 1024

NA_W = NA_HEADS * HEAD_DIM
GQ_W = GQ_HEADS * HEAD_DIM
GQ_KV_W = GQ_KV_HEADS * HEAD_DIM
OFF_NA_K = 0
OFF_NA_V = OFF_NA_K + NA_W
OFF_GQ_K = OFF_NA_V + NA_W
OFF_GQ_V = OFF_GQ_K + GQ_KV_W
OFF_SU = OFF_GQ_V + GQ_KV_W
CTX_IN_W = OFF_SU + SSM_WIDTH
OFF_NA_Q = CTX_IN_W
OFF_GQ_Q = OFF_NA_Q + NA_W
OFF_GATE = OFF_GQ_Q + GQ_W


def _tile(n, pref):
    t = min(n, pref)
    while n % t:
        t //= 2
    return t


def _params(*sem):
    return pltpu.CompilerParams(dimension_semantics=sem, vmem_limit_bytes=VMEM_LIMIT)


def _sigmoid(x):
    return 1.0 / (1.0 + jnp.exp(-x))


def _modvec_kernel(c_ref, w_ref, b_ref, o_ref):
    c = c_ref[...]
    a = (c * _sigmoid(c)).astype(BF16)
    o_ref[...] = jnp.dot(a, w_ref[...].astype(BF16), preferred_element_type=F32) + b_ref[...]


def _modvec(c_all, w, b):
    R, D = c_all.shape
    N = w.shape[1]
    tn = _tile(N, 1024)
    return pl.pallas_call(
        _modvec_kernel,
        grid=(N // tn,),
        in_specs=[pl.BlockSpec((R, D), lambda j: (0, 0)),
                  pl.BlockSpec((D, tn), lambda j: (0, j)),
                  pl.BlockSpec((1, tn), lambda j: (0, j))],
        out_specs=pl.BlockSpec((R, tn), lambda j: (0, j)),
        out_shape=jax.ShapeDtypeStruct((R, N), F32),
        compiler_params=_params("arbitrary"),
        name="modvec",
    )(c_all, w, b.reshape(1, N))


def _modulate_kernel(h_ref, sh_ref, sc_ref, o_ref):
    o_ref[0] = (h_ref[0] * (1.0 + sc_ref[0]) + sh_ref[0]).astype(o_ref.dtype)


def _modulate(h, mod3, row_fn, k_shift, k_scale):
    B, S, D = h.shape
    ts = _tile(S, 512)
    return pl.pallas_call(
        _modulate_kernel,
        grid=(B, S // ts),
        in_specs=[pl.BlockSpec((1, ts, D), lambda b, i: (b, i, 0)),
                  pl.BlockSpec((1, 1, D), lambda b, i: (row_fn(b), 0, k_shift)),
                  pl.BlockSpec((1, 1, D), lambda b, i: (row_fn(b), 0, k_scale))],
        out_specs=pl.BlockSpec((1, ts, D), lambda b, i: (b, i, 0)),
        out_shape=jax.ShapeDtypeStruct((B, S, D), BF16),
        compiler_params=_params("parallel", "parallel"),
        name="modulate",
    )(h, mod3, mod3)


def _mm_kernel(x_ref, w_ref, o_ref):
    o_ref[...] = jnp.dot(x_ref[...], w_ref[...], preferred_element_type=F32).astype(o_ref.dtype)


def _matmul(x, w, out_dtype, tm=1024, tn=1024):
    M, K = x.shape
    N = w.shape[1]
    tm, tn = _tile(M, tm), _tile(N, tn)
    return pl.pallas_call(
        _mm_kernel,
        grid=(M // tm, N // tn),
        in_specs=[pl.BlockSpec((tm, K), lambda i, j: (i, 0)),
                  pl.BlockSpec((K, tn), lambda i, j: (0, j))],
        out_specs=pl.BlockSpec((tm, tn), lambda i, j: (i, j)),
        out_shape=jax.ShapeDtypeStruct((M, N), out_dtype),
        compiler_params=_params("parallel", "arbitrary"),
        name="matmul",
    )(x, w)


def _nt_dot(a, b):
    return lax.dot_general(a, b, (((1,), (1,)), ((), ())), preferred_element_type=F32)


def _na_kernel(q_ref, k_ref, v_ref, kc_ref, vc_ref, tab_ref, o_ref, *, rows):
    kc = kc_ref[0]
    vc = vc_ref[0]
    win = NA_WIN_ROWS * GRID_W

    def body(r, carry):
        rs = jnp.clip(r - NA_WIN_ROWS // 2, 0, rows - NA_WIN_ROWS)
        j0 = rs - r + NA_WIN_ROWS - 1
        q = q_ref[0, pl.ds(pl.multiple_of(r * GRID_W, GRID_W), GRID_W), :]
        k0 = pl.multiple_of(rs * GRID_W, GRID_W)
        kb = k_ref[0, pl.ds(k0, win), :]
        vb = v_ref[0, pl.ds(k0, win), :]
        s1 = _nt_dot(q, kb) * ATT_SCALE + tab_ref[0, j0]
        s2 = _nt_dot(q, kc) * ATT_SCALE
        m = jnp.maximum(jnp.max(s1, axis=-1, keepdims=True), jnp.max(s2, axis=-1, keepdims=True))
        p1 = jnp.exp(s1 - m)
        p2 = jnp.exp(s2 - m)
        l = jnp.sum(p1, axis=-1, keepdims=True) + jnp.sum(p2, axis=-1, keepdims=True)
        o = (jnp.dot(p1.astype(BF16), vb, preferred_element_type=F32)
             + jnp.dot(p2.astype(BF16), vc, preferred_element_type=F32))
        o_ref[0, pl.ds(pl.multiple_of(r * GRID_W, GRID_W), GRID_W), :] = (o / l).astype(o_ref.dtype)
        return carry

    lax.fori_loop(0, rows, body, 0)


def _na_attention(z3, zc3, tab):
    B, S, _ = z3.shape
    L = zc3.shape[1]
    rows = S // GRID_W
    assert rows >= NA_WIN_ROWS
    hb = HEAD_DIM // LANES
    blk = lambda n, off: pl.BlockSpec((1, n, HEAD_DIM), lambda b, h: (b, 0, off // HEAD_DIM + h))
    return pl.pallas_call(
        functools.partial(_na_kernel, rows=rows),
        grid=(B, NA_HEADS),
        in_specs=[blk(S, OFF_NA_Q), blk(S, OFF_NA_K), blk(S, OFF_NA_V),
                  blk(L, OFF_NA_K), blk(L, OFF_NA_V),
                  pl.BlockSpec((1, NA_WIN_ROWS, GRID_W, NA_WIN_ROWS * GRID_W), lambda b, h: (h, 0, 0, 0))],
        out_specs=pl.BlockSpec((1, S, HEAD_DIM), lambda b, h: (b, 0, h)),
        out_shape=jax.ShapeDtypeStruct((B, S, NA_W), BF16),
        compiler_params=_params("parallel", "parallel"),
        name="na_attention",
    )(z3, z3, z3, zc3, zc3, tab)


def _na_bias_table(rpb):
    col = jnp.arange(GRID_W, dtype=jnp.int32)
    col_start = jnp.clip(col - NA_WIN_COLS // 2, 0, GRID_W - NA_WIN_COLS)
    col_mask = (col[None, :] >= col_start[:, None]) & (col[None, :] < col_start[:, None] + NA_WIN_COLS)
    col_idx = jnp.clip(col[None, :] - col[:, None], -(NA_WIN_COLS - 1), NA_WIN_COLS - 1) + NA_WIN_COLS - 1
    row_idx = jnp.arange(NA_WIN_ROWS)[:, None] + jnp.arange(NA_WIN_ROWS)[None, :]
    t = rpb.astype(F32)[:, row_idx][:, :, :, col_idx]
    t = jnp.where(col_mask[None, None, None], t, NEG)
    t = t.transpose(0, 1, 3, 2, 4)
    return t.reshape(NA_HEADS, NA_WIN_ROWS, GRID_W, NA_WIN_ROWS * GRID_W)


def _rms(x, g):
    return x * lax.rsqrt(jnp.mean(x * x, axis=-1, keepdims=True) + RMS_EPS) * g


def _rope(x, cos, sin_signed):
    lane = lax.broadcasted_iota(jnp.int32, x.shape, 1)
    nxt = pltpu.roll(x, HEAD_DIM - 1, 1)
    prv = pltpu.roll(x, 1, 1)
    partner = jnp.where((lane & 1) == 0, nxt, prv)
    return x * cos + partner * sin_signed


def _gqa_kernel(q0_ref, q1_ref, q2_ref, k_ref, v_ref, kc_ref, vc_ref, cosk_ref, sink_ref,
                cosq_ref, sinq_ref, qn_ref, kn_ref, o_ref, ks, vs, *, S, L):
    @pl.when(pl.program_id(2) == 0)
    def _():
        k = _rope(_rms(k_ref[0].astype(F32), kn_ref[...]), cosk_ref[...], sink_ref[...])
        ks[0:S, :] = k.astype(BF16)
        ks[S:S + L, :] = _rms(kc_ref[0].astype(F32), kn_ref[...]).astype(BF16)
        vs[0:S, :] = v_ref[0]
        vs[S:S + L, :] = vc_ref[0]

    for g, q_ref in enumerate((q0_ref, q1_ref, q2_ref)):
        q = _rope(_rms(q_ref[0].astype(F32), qn_ref[...]), cosq_ref[...], sinq_ref[...]) * ATT_SCALE
        s = _nt_dot(q.astype(BF16), ks[...])
        m = jnp.max(s, axis=-1, keepdims=True)
        p = jnp.exp(s - m)
        l = jnp.sum(p, axis=-1, keepdims=True)
        o = jnp.dot(p.astype(BF16), vs[...], preferred_element_type=F32) / l
        o_ref[0, :, g * HEAD_DIM:(g + 1) * HEAD_DIM] = o.astype(o_ref.dtype)


def _gqa_attention(z3, zc3, cos, sin, qn, kn):
    B, S, _ = z3.shape
    L = zc3.shape[1]
    tq = _tile(S, 512)
    qspec = lambda g: pl.BlockSpec(
        (1, tq, HEAD_DIM), lambda b, h, i: (b, i, OFF_GQ_Q // HEAD_DIM + h * GQ_GROUP + g))
    kv = lambda n, off: pl.BlockSpec((1, n, HEAD_DIM), lambda b, h, i: (b, 0, off // HEAD_DIM + h))
    full = pl.BlockSpec((S, HEAD_DIM), lambda b, h, i: (0, 0))
    tile = pl.BlockSpec((tq, HEAD_DIM), lambda b, h, i: (i, 0))
    vec = pl.BlockSpec((1, HEAD_DIM), lambda b, h, i: (0, 0))
    return pl.pallas_call(
        functools.partial(_gqa_kernel, S=S, L=L),
        grid=(B, GQ_KV_HEADS, S // tq),
        in_specs=[qspec(0), qspec(1), qspec(2), kv(S, OFF_GQ_K), kv(S, OFF_GQ_V),
                  kv(L, OFF_GQ_K), kv(L, OFF_GQ_V), full, full, tile, tile, vec, vec],
        out_specs=pl.BlockSpec((1, tq, GQ_GROUP * HEAD_DIM), lambda b, h, i: (b, i, h)),
        out_shape=jax.ShapeDtypeStruct((B, S, GQ_W), BF16),
        scratch_shapes=[pltpu.VMEM((S + L, HEAD_DIM), BF16), pltpu.VMEM((S + L, HEAD_DIM), BF16)],
        compiler_params=_params("parallel", "parallel", "arbitrary"),
        name="gqa_attention",
    )(z3, z3, z3, z3, z3, zc3, zc3, cos, sin, cos, sin, qn, kn)


def _rope_tables(S):
    t = jnp.arange(S, dtype=jnp.int32)
    row = (t // GRID_W).astype(F32)
    col = (t % GRID_W).astype(F32)
    axis_dim = HEAD_DIM // 2
    inv_freq = 1.0 / (ROPE_THETA ** (jnp.arange(0, axis_dim, 2, dtype=F32) / axis_dim))
    ang = jnp.concatenate([row[:, None] * inv_freq, col[:, None] * inv_freq], axis=-1)
    cos = jnp.repeat(jnp.cos(ang), 2, axis=-1)
    sin = jnp.repeat(jnp.sin(ang), 2, axis=-1)
    sign = jnp.where(jnp.arange(HEAD_DIM) % 2 == 0, -1.0, 1.0).astype(F32)
    return cos, sin * sign


def _ctx_attn_kernel(q_ref, k_ref, v_ref, qn_ref, kn_ref, o_ref, *, norm):
    q = q_ref[0]
    k = k_ref[0]
    if norm:
        q = _rms(q.astype(F32), qn_ref[...]).astype(BF16)
        k = _rms(k.astype(F32), kn_ref[...]).astype(BF16)
    s = _nt_dot(q, k) * ATT_SCALE
    m = jnp.max(s, axis=-1, keepdims=True)
    p = jnp.exp(s - m)
    l = jnp.sum(p, axis=-1, keepdims=True)
    o_ref[0] = (jnp.dot(p.astype(BF16), v_ref[0], preferred_element_type=F32) / l).astype(o_ref.dtype)


def _ctx_attention(zc3, off_q, off_k, off_v, group, norm, qn, kn):
    B, L, _ = zc3.shape
    n_heads = NA_HEADS
    spec = lambda off, div: pl.BlockSpec((1, L, HEAD_DIM), lambda b, h: (b, 0, off // HEAD_DIM + h // div))
    vec = pl.BlockSpec((1, HEAD_DIM), lambda b, h: (0, 0))
    return pl.pallas_call(
        functools.partial(_ctx_attn_kernel, norm=norm),
        grid=(B, n_heads),
        in_specs=[spec(off_q, 1), spec(off_k, group), spec(off_v, group), vec, vec],
        out_specs=pl.BlockSpec((1, L, HEAD_DIM), lambda b, h: (b, 0, h)),
        out_shape=jax.ShapeDtypeStruct((B, L, n_heads * HEAD_DIM), BF16),
        compiler_params=_params("parallel", "parallel"),
        name="ctx_attention",
    )(zc3, zc3, zc3, qn, kn)


SSM_SLABS = SSM_WIDTH // LANES
SLAB_STATE = SSM_LANES // SSM_SLABS


def _s5_kernel(uf_ref, ub_ref, wb_ref, wc_ref, a_ref, h0_ref, yf_ref, yb_ref, hout_ref,
               br, bi, hs, *, tc, n, nb):
    j = pl.program_id(0)
    rows = tc * nb

    @pl.when(j == 0)
    def _():
        hs[...] = h0_ref[...]

    for d, (u_ref, y_ref) in enumerate(((uf_ref, yf_ref), (ub_ref, yb_ref))):
        u = u_ref[...].reshape(rows, SSM_WIDTH).astype(BF16)
        for k in range(SSM_SLABS):
            r = jnp.dot(u[:, k * LANES:(k + 1) * LANES], wb_ref[d, k], preferred_element_type=F32)
            br[:, k * SLAB_STATE:(k + 1) * SLAB_STATE] = r[:, :SLAB_STATE]
            bi[:, k * SLAB_STATE:(k + 1) * SLAB_STATE] = r[:, SLAB_STATE:]
        ar = jnp.broadcast_to(a_ref[d, 0], (nb, SSM_LANES))
        ai = jnp.broadcast_to(a_ref[d, 1], (nb, SSM_LANES))

        def step(i, carry, d=d, ar=ar, ai=ai):
            hr, hi = carry
            t = i if d == 0 else tc - 1 - i
            row = pl.multiple_of(t * nb, nb)
            nr = ar * hr - ai * hi + br[pl.ds(row, nb), :]
            ni = ar * hi + ai * hr + bi[pl.ds(row, nb), :]
            br[pl.ds(row, nb), :] = nr
            bi[pl.ds(row, nb), :] = ni
            return nr, ni

        hr, hi = lax.fori_loop(0, tc, step, (hs[2 * d], hs[2 * d + 1]))
        hs[2 * d] = hr
        hs[2 * d + 1] = hi
        for k in range(SSM_SLABS):
            sl = slice(k * SLAB_STATE, (k + 1) * SLAB_STATE)
            y = (jnp.dot(br[:, sl].astype(BF16), wc_ref[d, k, 0:SLAB_STATE, :], preferred_element_type=F32)
                 + jnp.dot(bi[:, sl].astype(BF16), wc_ref[d, k, SLAB_STATE:, :], preferred_element_type=F32))
            y_ref[:, :, k * LANES:(k + 1) * LANES] = y.reshape(tc, nb, LANES)

    @pl.when(j == n - 1)
    def _():
        hout_ref[...] = hs[...]


def _s5_scan(u_t, wb, wc, a, h0):
    T, nb, _ = u_t.shape
    tc = _tile(T, 64)
    n = T // tc
    ublk = lambda rev: pl.BlockSpec((tc, nb, SSM_WIDTH), (lambda j: (n - 1 - j, 0, 0)) if rev else (lambda j: (j, 0, 0)))
    const = lambda shape: pl.BlockSpec(shape, lambda j: (0,) * len(shape))
    return pl.pallas_call(
        functools.partial(_s5_kernel, tc=tc, n=n, nb=nb),
        grid=(n,),
        in_specs=[ublk(False), ublk(True), const(wb.shape), const(wc.shape), const(a.shape), const(h0.shape)],
        out_specs=[ublk(False), ublk(True), const(h0.shape)],
        out_shape=[jax.ShapeDtypeStruct((T, nb, SSM_WIDTH), F32), jax.ShapeDtypeStruct((T, nb, SSM_WIDTH), F32),
                   jax.ShapeDtypeStruct(h0.shape, F32)],
        scratch_shapes=[pltpu.VMEM((tc * nb, SSM_LANES), F32), pltpu.VMEM((tc * nb, SSM_LANES), F32),
                        pltpu.VMEM(h0.shape, F32)],
        compiler_params=_params("arbitrary"),
        name="s5_scan",
    )(u_t, u_t, wb, wc, a, h0)


def _s5_weights(lam_r, lam_i, log_dt, b_r, b_i, c_r, c_i):
    lam_r, lam_i, log_dt = lam_r.astype(F32), lam_i.astype(F32), log_dt.astype(F32)
    dt = jnp.exp(log_dt)[..., None]
    mag = jnp.exp(lam_r * dt)
    ar, ai = mag * jnp.cos(lam_i * dt), mag * jnp.sin(lam_i * dt)
    den = lam_r * lam_r + lam_i * lam_i
    nr, ni = ar - 1.0, ai
    kr, ki = (nr * lam_r + ni * lam_i) / den, (ni * lam_r - nr * lam_i) / den
    b_r, b_i = b_r.astype(F32), b_i.astype(F32)
    bbr = kr[..., None] * b_r - ki[..., None] * b_i
    bbi = kr[..., None] * b_i + ki[..., None] * b_r
    gs = SSM_GROUPS // SSM_SLABS
    eye = jnp.eye(gs, dtype=F32)

    def in_slab(w):
        w = w.reshape(2, SSM_SLABS, gs, SSM_STATE, SSM_GROUP)
        return jnp.einsum('dkgph,gj->dkghjp', w, eye).reshape(2, SSM_SLABS, gs * SSM_GROUP, gs * SSM_STATE)

    def out_slab(w):
        w = w.reshape(2, SSM_SLABS, gs, SSM_GROUP, SSM_STATE)
        return jnp.einsum('dkghp,gj->dkjpgh', w, eye).reshape(2, SSM_SLABS, gs * SSM_STATE, gs * SSM_GROUP)

    wb = jnp.concatenate([in_slab(bbr), in_slab(bbi)], axis=-1).astype(BF16)
    wc = jnp.concatenate([out_slab(c_r.astype(F32)), -out_slab(c_i.astype(F32))], axis=-2).astype(BF16)
    a = jnp.stack([ar.reshape(2, 1, SSM_LANES), ai.reshape(2, 1, SSM_LANES)], axis=1)
    return wb, wc, a


def _gelu_tanh(x):
    return 0.5 * x * (1.0 + jnp.tanh(math.sqrt(2.0 / math.pi) * (x + 0.044715 * (x * x * x))))


def _glu_kernel(u_ref, yf_ref, yb_ref, d_ref, w_ref, b_ref, o_ref):
    y = u_ref[0].astype(F32) * d_ref[...] + yf_ref[...] + yb_ref[...]
    t = _gelu_tanh(y).astype(BF16)
    r = jnp.dot(t, w_ref[...], preferred_element_type=F32) + b_ref[...]
    o_ref[0] = (r[:, :SSM_WIDTH] * _sigmoid(r[:, SSM_WIDTH:])).astype(o_ref.dtype)


def _glu(z3, yf, yb, d, w, b):
    B, S, _ = z3.shape
    ts = _tile(S, 512)
    y2 = lambda y: y.reshape(S, B * SSM_WIDTH)
    return pl.pallas_call(
        _glu_kernel,
        grid=(B, S // ts),
        in_specs=[pl.BlockSpec((1, ts, SSM_WIDTH), lambda b, i: (b, i, OFF_SU // SSM_WIDTH)),
                  pl.BlockSpec((ts, SSM_WIDTH), lambda b, i: (i, b)),
                  pl.BlockSpec((ts, SSM_WIDTH), lambda b, i: (i, b)),
                  pl.BlockSpec((1, SSM_WIDTH), lambda b, i: (0, 0)),
                  pl.BlockSpec((SSM_WIDTH, 2 * SSM_WIDTH), lambda b, i: (0, 0)),
                  pl.BlockSpec((1, 2 * SSM_WIDTH), lambda b, i: (0, 0))],
        out_specs=pl.BlockSpec((1, ts, SSM_WIDTH), lambda b, i: (b, i, 0)),
        out_shape=jax.ShapeDtypeStruct((B, S, SSM_WIDTH), BF16),
        compiler_params=_params("parallel", "parallel"),
        name="s5_glu",
    )(z3, y2(yf), y2(yb), d, w, b)


def _merge_kernel(a_ref, b_ref, s_ref, ga_ref, gb_ref, gs_ref, wa_ref, wb_ref, ws_ref, o_ref):
    m = (_sigmoid(ga_ref[...].astype(F32)) * jnp.dot(a_ref[...], wa_ref[...], preferred_element_type=F32)
         + _sigmoid(gb_ref[...].astype(F32)) * jnp.dot(b_ref[...], wb_ref[...], preferred_element_type=F32)
         + _sigmoid(gs_ref[...].astype(F32)) * jnp.dot(s_ref[...], ws_ref[...], preferred_element_type=F32))
    o_ref[...] = m.astype(o_ref.dtype)


def _merge(a, b, s, z2, wa, wb, ws):
    M = a.shape[0]
    D = wa.shape[1]
    tm, tn = _tile(M, 1024), _tile(D, 1024)
    row = lambda w: pl.BlockSpec((tm, w), lambda i, j: (i, 0))
    gate = lambda br: pl.BlockSpec((tm, tn), lambda i, j: (i, (OFF_GATE + br * D) // tn + j))
    wsp = lambda w: pl.BlockSpec((w, tn), lambda i, j: (0, j))
    return pl.pallas_call(
        _merge_kernel,
        grid=(M // tm, D // tn),
        in_specs=[row(NA_W), row(GQ_W), row(SSM_WIDTH), gate(0), gate(1), gate(2),
                  wsp(NA_W), wsp(GQ_W), wsp(SSM_WIDTH)],
        out_specs=pl.BlockSpec((tm, tn), lambda i, j: (i, j)),
        out_shape=jax.ShapeDtypeStruct((M, D), BF16),
        compiler_params=_params("parallel", "arbitrary"),
        name="merge",
    )(a, b, s, z2, z2, z2, wa, wb, ws)


def _layer_norm(x, g, b):
    mu = jnp.mean(x, axis=-1, keepdims=True)
    xc = x - mu
    var = jnp.mean(xc * xc, axis=-1, keepdims=True)
    return xc * lax.rsqrt(var + LN_EPS) * g + b


def _outproj_kernel(m_ref, w_ref, h_ref, g1_ref, sh_ref, sc_ref, lng_ref, lnb_ref, rwh_ref, rwl_ref, rb_ref,
                    hn_ref, tokr_ref, tokb_ref, idx_ref, wts_ref, *, tm, D):
    mix = jnp.dot(m_ref[...], w_ref[...], preferred_element_type=F32)
    hn = _layer_norm(DN_ALPHA * h_ref[...] + g1_ref[0] * mix, lng_ref[...], lnb_ref[...])
    hn_ref[...] = hn
    tok = hn * (1.0 + sc_ref[0]) + sh_ref[0]
    for r in range(D // LANES):
        tokr_ref[pl.ds(r, tm, stride=D // LANES), :] = tok[:, r * LANES:(r + 1) * LANES]
    t_hi = tok.astype(BF16)
    tokb_ref[...] = t_hi
    t_lo = (tok - t_hi.astype(F32)).astype(BF16)
    logits = (jnp.dot(t_hi, rwh_ref[...], preferred_element_type=F32)
              + jnp.dot(t_lo, rwh_ref[...], preferred_element_type=F32)
              + jnp.dot(t_hi, rwl_ref[...], preferred_element_type=F32))
    scores = _sigmoid(logits)
    sel = scores + rb_ref[...]
    lane = lax.broadcasted_iota(jnp.int32, sel.shape, 1).astype(F32)
    slot = lax.broadcasted_iota(jnp.int32, (tm, LANES), 1)
    idx_acc = jnp.zeros((tm, LANES), F32)
    w_acc = jnp.zeros((tm, LANES), F32)
    for k in range(TOP_K):
        mx = jnp.max(sel, axis=-1, keepdims=True)
        am = jnp.min(jnp.where(sel == mx, lane, float(N_EXPERTS)), axis=-1, keepdims=True)
        hit = lane == am
        wk = jnp.sum(jnp.where(hit, scores, 0.0), axis=-1, keepdims=True)
        idx_acc = jnp.where(slot == k, am, idx_acc)
        w_acc = jnp.where(slot == k, wk, w_acc)
        sel = jnp.where(hit, -jnp.inf, sel)
    wsum = jnp.sum(w_acc, axis=-1, keepdims=True)
    idx_ref[...] = idx_acc.astype(jnp.int32)
    wts_ref[...] = w_acc / wsum * ROUTED_SCALE


def _outproj_ln_route(m, w_out, h2, mod3, row_fn, ln_g, ln_b, rw_hi, rw_lo, rbias, tm=256):
    M, D = m.shape
    tm = _tile(M, tm)
    nr = D // LANES
    vec = lambda k: pl.BlockSpec((1, 1, D), lambda i: (row_fn(i), 0, k))
    const = lambda shape: pl.BlockSpec(shape, lambda i: (0,) * len(shape))
    rowblk = lambda w: pl.BlockSpec((tm, w), lambda i: (i, 0))
    return pl.pallas_call(
        functools.partial(_outproj_kernel, tm=tm, D=D),
        grid=(M // tm,),
        in_specs=[rowblk(D), const((D, D)), rowblk(D), vec(2), vec(3), vec(4), const((1, D)), const((1, D)),
                  const((D, N_EXPERTS)), const((D, N_EXPERTS)), const((1, N_EXPERTS))],
        out_specs=[rowblk(D), pl.BlockSpec((tm * nr, LANES), lambda i: (i, 0)), rowblk(D),
                   rowblk(LANES), rowblk(LANES)],
        out_shape=[jax.ShapeDtypeStruct((M, D), F32), jax.ShapeDtypeStruct((M * nr, LANES), F32),
                   jax.ShapeDtypeStruct((M, D), BF16), jax.ShapeDtypeStruct((M, LANES), jnp.int32),
                   jax.ShapeDtypeStruct((M, LANES), F32)],
        compiler_params=_params("parallel"),
        name="outproj_ln_route",
    )(m, w_out, h2, mod3, mod3, mod3, ln_g, ln_b, rw_hi, rw_lo, rbias)


def _swiglu_kernel(x_ref, w13_ref, w2_ref, o_ref, *, E):
    h = jnp.dot(x_ref[...], w13_ref[...], preferred_element_type=F32)
    g = h[:, :E]
    act = (g * _sigmoid(g) * h[:, E:]).astype(BF16)
    o_ref[...] = jnp.dot(act, w2_ref[...], preferred_element_type=F32).astype(o_ref.dtype)


def _shared_expert(tokb, w13, w2):
    T, D = tokb.shape
    E = w2.shape[0]
    tm = _tile(T, 512)
    return pl.pallas_call(
        functools.partial(_swiglu_kernel, E=E),
        grid=(T // tm,),
        in_specs=[pl.BlockSpec((tm, D), lambda i: (i, 0)), pl.BlockSpec((D, 2 * E), lambda i: (0, 0)),
                  pl.BlockSpec((E, D), lambda i: (0, 0))],
        out_specs=pl.BlockSpec((tm, D), lambda i: (i, 0)),
        out_shape=jax.ShapeDtypeStruct((T, D), BF16),
        compiler_params=_params("parallel"),
        name="shared_expert",
    )(tokb, w13, w2)


def _expert_kernel(be_ref, src_ref, nsrc_ref, dst_ref, sw_ref, tok_hbm, w1_ref, w3_ref, w2_ref, y_hbm,
                   xbuf, ybuf, w13s, w2s, gsem, ssem, *, tm, n, D, E):
    i = pl.program_id(0)
    nr = D // LANES
    slab = tm * nr

    def gather(slot, idx_ref):
        def body(r, c):
            t = idx_ref[0, 0, r]
            pltpu.make_async_copy(tok_hbm.at[pl.ds(pl.multiple_of(t * nr, nr), nr)],
                                  xbuf.at[pl.ds(pl.multiple_of(slot * slab + r * nr, nr), nr)],
                                  gsem.at[slot]).start()
            return c
        lax.fori_loop(0, tm, body, 0)

    def gather_wait(slot):
        pltpu.make_async_copy(tok_hbm.at[pl.ds(0, slab)],
                              xbuf.at[pl.ds(pl.multiple_of(slot * slab, slab), slab)], gsem.at[slot]).wait()

    def scatter_wait():
        pltpu.make_async_copy(ybuf, y_hbm.at[pl.ds(0, slab)], ssem.at[0]).wait()

    @pl.when(i == 0)
    def _():
        gather(0, src_ref)

    @pl.when(i + 1 < n)
    def _():
        gather((i + 1) % 2, nsrc_ref)

    @pl.when((i == 0) | (be_ref[i] != be_ref[jnp.maximum(i - 1, 0)]))
    def _():
        w13s[:, 0:E] = w1_ref[0].astype(BF16)
        w13s[:, E:2 * E] = w3_ref[0].astype(BF16)
        w2s[...] = w2_ref[0].astype(BF16)

    slot = i % 2
    gather_wait(slot)
    base = slot * slab
    x = jnp.concatenate([xbuf[pl.ds(base + r, tm, stride=nr), :] for r in range(nr)], axis=1).astype(BF16)
    h = jnp.dot(x, w13s[...], preferred_element_type=F32)
    g = h[:, :E]
    act = (g * _sigmoid(g) * h[:, E:]).astype(BF16)
    y = jnp.dot(act, w2s[...], preferred_element_type=F32) * sw_ref[0]

    @pl.when(i > 0)
    def _():
        scatter_wait()

    for r in range(nr):
        ybuf[pl.ds(r, tm, stride=nr), :] = y[:, r * LANES:(r + 1) * LANES]

    def sbody(r, c):
        t = dst_ref[0, 0, r]
        pltpu.make_async_copy(ybuf.at[pl.ds(pl.multiple_of(r * nr, nr), nr)],
                              y_hbm.at[pl.ds(pl.multiple_of(t * nr, nr), nr)], ssem.at[0]).start()
        return c
    lax.fori_loop(0, tm, sbody, 0)

    @pl.when(i == n - 1)
    def _():
        scatter_wait()


def _routed_experts(tokr, blk_exp, slot_src, slot_dst, slot_w, w1, w3, w2, tm):
    n = blk_exp.shape[0]
    _, D, E = w1.shape
    nr = D // LANES
    cap = n * tm
    idx3 = lambda a: a.reshape(n, 1, tm)
    smem_blk = lambda fn: pl.BlockSpec((1, 1, tm), fn, memory_space=pltpu.SMEM)
    grid_spec = pltpu.PrefetchScalarGridSpec(
        num_scalar_prefetch=1,
        grid=(n,),
        in_specs=[smem_blk(lambda i, be: (i, 0, 0)),
                  smem_blk(lambda i, be: (jnp.minimum(i + 1, n - 1), 0, 0)),
                  smem_blk(lambda i, be: (i, 0, 0)),
                  pl.BlockSpec((1, tm, 1), lambda i, be: (i, 0, 0)),
                  pl.BlockSpec(memory_space=pl.ANY),
                  pl.BlockSpec((1, D, E), lambda i, be: (be[i], 0, 0)),
                  pl.BlockSpec((1, D, E), lambda i, be: (be[i], 0, 0)),
                  pl.BlockSpec((1, E, D), lambda i, be: (be[i], 0, 0))],
        out_specs=pl.BlockSpec(memory_space=pl.ANY),
        scratch_shapes=[pltpu.VMEM((2 * tm * nr, LANES), F32), pltpu.VMEM((tm * nr, LANES), F32),
                        pltpu.VMEM((D, 2 * E), BF16), pltpu.VMEM((E, D), BF16),
                        pltpu.SemaphoreType.DMA((2,)), pltpu.SemaphoreType.DMA((1,))],
    )
    return pl.pallas_call(
        functools.partial(_expert_kernel, tm=tm, n=n, D=D, E=E),
        grid_spec=grid_spec,
        out_shape=jax.ShapeDtypeStruct((cap * nr, LANES), F32),
        compiler_params=pltpu.CompilerParams(dimension_semantics=("arbitrary",), vmem_limit_bytes=VMEM_LIMIT,
                                             disable_bounds_checks=True),
        name="routed_experts",
    )(blk_exp, idx3(slot_src), idx3(slot_src), idx3(slot_dst), slot_w.reshape(n, tm, 1), tokr, w1, w3, w2)


def _route_slots(idx, wts, tm):
    T = idx.shape[0]
    A = T * TOP_K
    e_flat = idx.reshape(A)
    w_flat = wts.reshape(A)
    order = jnp.argsort(e_flat).astype(jnp.int32)
    e_s = e_flat[order]
    counts = jnp.bincount(e_flat, length=N_EXPERTS).astype(jnp.int32)
    padded = (counts + tm - 1) // tm * tm
    pend = jnp.cumsum(padded)
    pstart = pend - padded
    ostart = jnp.cumsum(counts) - counts
    dest = pstart[e_s] + jnp.arange(A, dtype=jnp.int32) - ostart[e_s]
    n_blk = -(-(A + N_EXPERTS * (tm - 1)) // tm)
    cap = n_blk * tm
    slot_src = jnp.zeros((cap,), jnp.int32).at[dest].set(order // TOP_K)
    slot_w = jnp.zeros((cap,), F32).at[dest].set(w_flat[order])
    real = jnp.zeros((cap,), jnp.int32).at[dest].set(1)
    spare = A + jnp.cumsum(1 - real) - 1
    slot_dst = jnp.where(real == 1, jnp.zeros((cap,), jnp.int32).at[dest].set(order), spare).astype(jnp.int32)
    blk_exp = jnp.minimum(jnp.searchsorted(pend, jnp.arange(n_blk, dtype=jnp.int32) * tm, side='right'),
                          N_EXPERTS - 1).astype(jnp.int32)
    return blk_exp, slot_src, slot_dst, slot_w


def _combine_kernel(h_ref, sh_ref, y_ref, g2_ref, lng_ref, lnb_ref, o_ref, *, tm, D):
    nr = D // LANES
    parts = []
    for r in range(nr):
        acc = y_ref[pl.ds(r, tm, stride=TOP_K * nr), :]
        for k in range(1, TOP_K):
            acc = acc + y_ref[pl.ds(k * nr + r, tm, stride=TOP_K * nr), :]
        parts.append(acc)
    f = jnp.concatenate(parts, axis=1) + sh_ref[...].astype(F32)
    o_ref[...] = _layer_norm(DN_ALPHA * h_ref[...] + g2_ref[0] * f, lng_ref[...], lnb_ref[...])


def _combine_ln(h2, shared, y, tok_off, mod3, row_fn, ln_g, ln_b, tm=128):
    M, D = h2.shape
    tm = _tile(M, tm)
    nr = D // LANES
    off = tok_off // tm
    rowblk = lambda o: pl.BlockSpec((tm, D), lambda i: (i + o, 0))
    const = lambda shape: pl.BlockSpec(shape, lambda i: (0,) * len(shape))
    return pl.pallas_call(
        functools.partial(_combine_kernel, tm=tm, D=D),
        grid=(M // tm,),
        in_specs=[rowblk(0), rowblk(off), pl.BlockSpec((tm * TOP_K * nr, LANES), lambda i: (i + off, 0)),
                  pl.BlockSpec((1, 1, D), lambda i: (row_fn(i), 0, 5)), const((1, D)), const((1, D))],
        out_specs=rowblk(0),
        out_shape=jax.ShapeDtypeStruct((M, D), F32),
        compiler_params=_params("parallel"),
        name="combine_ln",
    )(h2, shared, y, mod3, ln_g, ln_b)


MOE_TM = 256


def kernel(x, c, ctx, c_ctx, w_mod, b_mod, w_in, na_rpb, gq_q_norm, gq_k_norm, ssm_lam_re, ssm_lam_im, ssm_log_dt, ssm_b_re, ssm_b_im, ssm_c_re, ssm_c_im, ssm_d, ssm_glu_w, ssm_glu_b, w_br_na, w_br_gq, w_br_ssm, w_out, ln1_g, ln1_b, router_w, router_bias, exp_w1, exp_w3, exp_w2, sh_w1, sh_w3, sh_w2, ln2_g, ln2_b):
    B, S, D = x.shape
    L = ctx.shape[1]
    assert B < 2 * SUBLANES and D % LANES == 0
    cos, sin = _rope_tables(S)
    c_all = jnp.zeros((2 * SUBLANES, D), F32).at[:B].set(c).at[B].set(c_ctx)
    lat_row = lambda b: b
    ctx_row = lambda b: B
    h, hc = x, ctx
    for l in range(DEPTH):
        need_ctx = l < DEPTH - 1
        mod3 = _modvec(c_all, w_mod[l], b_mod[l])[:, None, :]
        vec = lambda p: p[l].reshape(1, -1).astype(F32)

        w_in_b = w_in[l].astype(BF16)
        u = _modulate(h, mod3, lat_row, 0, 1)
        uc = _modulate(hc, mod3, ctx_row, 0, 1)
        z2 = _matmul(u.reshape(B * S, D), w_in_b, BF16)
        zc2 = _matmul(uc.reshape(B * L, D), w_in_b if need_ctx else w_in_b[:, :CTX_IN_W], BF16)
        z3, zc3 = z2.reshape(B, S, -1), zc2.reshape(B, L, -1)

        a_out = _na_attention(z3, zc3, _na_bias_table(na_rpb[l]))
        qn, kn = vec(gq_q_norm), vec(gq_k_norm)
        b_out = _gqa_attention(z3, zc3, cos, sin, qn, kn)

        wb, wc, a = _s5_weights(ssm_lam_re[l], ssm_lam_im[l], ssm_log_dt[l], ssm_b_re[l], ssm_b_im[l],
                                ssm_c_re[l], ssm_c_im[l])
        su = lambda t: t[:, :, OFF_SU:OFF_SU + SSM_WIDTH].astype(F32).transpose(1, 0, 2)
        h0 = jnp.zeros((4, B, SSM_LANES), F32)
        ycf, ycb, hfin = _s5_scan(su(zc3), wb, wc, a, h0)
        yf, yb, _ = _s5_scan(su(z3), wb, wc, a, hfin)
        glu_w, glu_b, dvec = ssm_glu_w[l].astype(BF16), vec(ssm_glu_b), vec(ssm_d)
        c_out = _glu(z3, yf, yb, dvec, glu_w, glu_b)

        wa, wg, ws, wo = (w[l].astype(BF16) for w in (w_br_na, w_br_gq, w_br_ssm, w_out))
        m = _merge(a_out.reshape(B * S, -1), b_out.reshape(B * S, -1), c_out.reshape(B * S, -1), z2, wa, wg, ws)
        rw = router_w[l].astype(F32)
        rw_hi = rw.astype(BF16)
        rw_lo = (rw - rw_hi.astype(F32)).astype(BF16)
        rbias = vec(router_bias)
        g1, b1 = vec(ln1_g), vec(ln1_b)
        tm_o = _tile(S, 256)
        h2, tokr, tokb, idx, wts = _outproj_ln_route(m, wo, h.reshape(B * S, D), mod3, lambda i: i // (S // tm_o),
                                                     g1, b1, rw_hi, rw_lo, rbias, tm=tm_o)
        if need_ctx:
            a_ctx = _ctx_attention(zc3, OFF_NA_Q, OFF_NA_K, OFF_NA_V, 1, False, qn, kn)
            b_ctx = _ctx_attention(zc3, OFF_GQ_Q, OFF_GQ_K, OFF_GQ_V, GQ_GROUP, True, qn, kn)
            c_ctx_out = _glu(zc3, ycf, ycb, dvec, glu_w, glu_b)
            mc = _merge(a_ctx.reshape(B * L, -1), b_ctx.reshape(B * L, -1), c_ctx_out.reshape(B * L, -1),
                        zc2, wa, wg, ws)
            tm_c = _tile(L, 256)
            hc2, tokr_c, tokb_c, idx_c, wts_c = _outproj_ln_route(
                mc, wo, hc.reshape(B * L, D), mod3, lambda i: B, g1, b1, rw_hi, rw_lo, rbias, tm=tm_c)
            tokr = jnp.concatenate([tokr, tokr_c], axis=0)
            tokb = jnp.concatenate([tokb, tokb_c], axis=0)
            idx = jnp.concatenate([idx, idx_c], axis=0)
            wts = jnp.concatenate([wts, wts_c], axis=0)

        w13 = jnp.concatenate([sh_w1[l], sh_w3[l]], axis=1).astype(BF16)
        shared = _shared_expert(tokb, w13, sh_w2[l].astype(BF16))
        blk_exp, slot_src, slot_dst, slot_w = _route_slots(idx[:, :TOP_K], wts[:, :TOP_K], MOE_TM)
        y = _routed_experts(tokr, blk_exp, slot_src, slot_dst, slot_w, exp_w1[l], exp_w3[l], exp_w2[l], MOE_TM)
        g2, b2 = vec(ln2_g), vec(ln2_b)
        tm_l = _tile(S, 128)
        h = _combine_ln(h2, shared, y, 0, mod3, lambda i: i // (S // tm_l), g2, b2, tm=tm_l).reshape(B, S, D)
        if need_ctx:
            hc = _combine_ln(hc2, shared, y, B * S, mod3, lambda i: B, g2, b2, tm=_tile(L, 128)).reshape(B, L, D)
    return h
```

```python
import functools
import math

import jax
import jax.numpy as jnp
import numpy as np
from jax import lax
from jax.experimental import pallas as pl
from jax.experimental.pallas import tpu as pltpu

F32 = jnp.float32
BF16 = jnp.bfloat16

DEPTH = 2
GRID_W = 64
HEAD_DIM = 128
NA_HEADS = 6
NA_WIN_ROWS = 8
NA_WIN_COLS = 16
GQ_HEADS = 6
GQ_KV_HEADS = 2
GQ_GROUP = GQ_HEADS // GQ_KV_HEADS
ROPE_THETA = 10000.0
SSM_GROUP = 16
SSM_WIDTH = 512
SSM_GROUPS = SSM_WIDTH // SSM_GROUP
SSM_STATE = 64
SSM_LANES = SSM_GROUPS * SSM_STATE
N_EXPERTS = 64
TOP_K = 8
EXPERT_DIM = 512
ROUTED_SCALE = 2.5
DN_ALPHA = (2 * DEPTH) ** 0.25
LN_EPS = 1e-6
RMS_EPS = 1e-6
ATT_SCALE = HEAD_DIM ** -0.5
NEG = -1e30
NA_ROW_UNROLL = 4

LANES = 128
SUBLANES = 8
VMEM_LIMIT = 56 * 1024 * 1024

NA_W = NA_HEADS * HEAD_DIM
GQ_W = GQ_HEADS * HEAD_DIM
GQ_KV_W = GQ_KV_HEADS * HEAD_DIM
OFF_NA_K = 0
OFF_NA_V = OFF_NA_K + NA_W
OFF_GQ_K = OFF_NA_V + NA_W
OFF_GQ_V = OFF_GQ_K + GQ_KV_W
OFF_SU = OFF_GQ_V + GQ_KV_W
CTX_IN_W = OFF_SU + SSM_WIDTH
OFF_NA_Q = CTX_IN_W
OFF_GQ_Q = OFF_NA_Q + NA_W
OFF_GATE = OFF_GQ_Q + GQ_W


def _tile(n, pref):
    t = min(n, pref)
    while n % t:
        t //= 2
    return t


def _params(*sem):
    return pltpu.CompilerParams(dimension_semantics=sem, vmem_limit_bytes=VMEM_LIMIT)


def _sigmoid(x):
    return 1.0 / (1.0 + jnp.exp(-x))


def _modvec_kernel(c_ref, w_ref, b_ref, o_ref):
    c = c_ref[...]
    a = (c * _sigmoid(c)).astype(BF16)
    o_ref[...] = jnp.dot(a, w_ref[...].astype(BF16), preferred_element_type=F32) + b_ref[...]


def _modvec(c_all, w, l, b):
    R, D = c_all.shape
    N = w.shape[2]
    tn = _tile(N, 1024)
    return pl.pallas_call(
        _modvec_kernel,
        grid=(N // tn,),
        in_specs=[pl.BlockSpec((R, D), lambda j: (0, 0)),
                  pl.BlockSpec((None, D, tn), lambda j: (l, 0, j)),
                  pl.BlockSpec((1, tn), lambda j: (0, j))],
        out_specs=pl.BlockSpec((R, tn), lambda j: (0, j)),
        out_shape=jax.ShapeDtypeStruct((R, N), F32),
        compiler_params=_params("arbitrary"),
        name="modvec",
    )(c_all, w, b.reshape(1, N))


def _modulate_kernel(h_ref, sh_ref, sc_ref, o_ref):
    o_ref[0] = (h_ref[0] * (1.0 + sc_ref[0]) + sh_ref[0]).astype(o_ref.dtype)


def _modulate(h, mod3, row_fn, k_shift, k_scale):
    B, S, D = h.shape
    ts = _tile(S, 512)
    return pl.pallas_call(
        _modulate_kernel,
        grid=(B, S // ts),
        in_specs=[pl.BlockSpec((1, ts, D), lambda b, i: (b, i, 0)),
                  pl.BlockSpec((1, 1, D), lambda b, i: (row_fn(b), 0, k_shift)),
                  pl.BlockSpec((1, 1, D), lambda b, i: (row_fn(b), 0, k_scale))],
        out_specs=pl.BlockSpec((1, ts, D), lambda b, i: (b, i, 0)),
        out_shape=jax.ShapeDtypeStruct((B, S, D), BF16),
        compiler_params=_params("parallel", "parallel"),
        name="modulate",
    )(h, mod3, mod3)


def _mm_kernel(x_ref, w_ref, o_ref):
    o_ref[...] = jnp.dot(x_ref[...], w_ref[...], preferred_element_type=F32).astype(o_ref.dtype)


def _matmul(x, w, out_dtype, tm=1024, tn=1024):
    M, K = x.shape
    N = w.shape[1]
    tm, tn = _tile(M, tm), _tile(N, tn)
    return pl.pallas_call(
        _mm_kernel,
        grid=(M // tm, N // tn),
        in_specs=[pl.BlockSpec((tm, K), lambda i, j: (i, 0)),
                  pl.BlockSpec((K, tn), lambda i, j: (0, j))],
        out_specs=pl.BlockSpec((tm, tn), lambda i, j: (i, j)),
        out_shape=jax.ShapeDtypeStruct((M, N), out_dtype),
        compiler_params=_params("parallel", "arbitrary"),
        name="matmul",
    )(x, w)


def _nt_dot(a, b):
    return lax.dot_general(a, b, (((1,), (1,)), ((), ())), preferred_element_type=F32)


def _na_kernel(q_ref, k_ref, v_ref, kc_ref, vc_ref, tab_ref, o_ref, *, rows):
    kc = kc_ref[0]
    vc = vc_ref[0]
    win = NA_WIN_ROWS * GRID_W

    def one_row(r):
        rs = jnp.clip(r - NA_WIN_ROWS // 2, 0, rows - NA_WIN_ROWS)
        j0 = rs - r + NA_WIN_ROWS - 1
        q = q_ref[0, pl.ds(pl.multiple_of(r * GRID_W, GRID_W), GRID_W), :]
        k0 = pl.multiple_of(rs * GRID_W, GRID_W)
        kb = k_ref[0, pl.ds(k0, win), :]
        vb = v_ref[0, pl.ds(k0, win), :]
        s1 = _nt_dot(q, kb) * ATT_SCALE + tab_ref[0, j0]
        s2 = _nt_dot(q, kc) * ATT_SCALE
        m = jnp.maximum(jnp.max(s1, axis=-1, keepdims=True), jnp.max(s2, axis=-1, keepdims=True))
        p1 = jnp.exp(s1 - m)
        p2 = jnp.exp(s2 - m)
        l = jnp.sum(p1, axis=-1, keepdims=True) + jnp.sum(p2, axis=-1, keepdims=True)
        o = (jnp.dot(p1.astype(BF16), vb, preferred_element_type=F32)
             + jnp.dot(p2.astype(BF16), vc, preferred_element_type=F32))
        o_ref[0, pl.ds(pl.multiple_of(r * GRID_W, GRID_W), GRID_W), :] = (o / l).astype(o_ref.dtype)

    def body(g, carry):
        for u in range(NA_ROW_UNROLL):
            one_row(g * NA_ROW_UNROLL + u)
        return carry

    lax.fori_loop(0, rows // NA_ROW_UNROLL, body, 0)


def _na_attention(z3, zc3, tab):
    B, S, _ = z3.shape
    L = zc3.shape[1]
    rows = S // GRID_W
    assert rows >= NA_WIN_ROWS and rows % NA_ROW_UNROLL == 0
    hb = HEAD_DIM // LANES
    blk = lambda n, off: pl.BlockSpec((1, n, HEAD_DIM), lambda b, h: (b, 0, off // HEAD_DIM + h))
    return pl.pallas_call(
        functools.partial(_na_kernel, rows=rows),
        grid=(B, NA_HEADS),
        in_specs=[blk(S, OFF_NA_Q), blk(S, OFF_NA_K), blk(S, OFF_NA_V),
                  blk(L, OFF_NA_K), blk(L, OFF_NA_V),
                  pl.BlockSpec((1, NA_WIN_ROWS, GRID_W, NA_WIN_ROWS * GRID_W), lambda b, h: (h, 0, 0, 0))],
        out_specs=pl.BlockSpec((1, S, HEAD_DIM), lambda b, h: (b, 0, h)),
        out_shape=jax.ShapeDtypeStruct((B, S, NA_W), BF16),
        compiler_params=_params("parallel", "parallel"),
        name="na_attention",
    )(z3, z3, z3, zc3, zc3, tab)


def _na_bias_table(rpb):
    col = jnp.arange(GRID_W, dtype=jnp.int32)
    col_start = jnp.clip(col - NA_WIN_COLS // 2, 0, GRID_W - NA_WIN_COLS)
    col_mask = (col[None, :] >= col_start[:, None]) & (col[None, :] < col_start[:, None] + NA_WIN_COLS)
    col_idx = jnp.clip(col[None, :] - col[:, None], -(NA_WIN_COLS - 1), NA_WIN_COLS - 1) + NA_WIN_COLS - 1
    row_idx = jnp.arange(NA_WIN_ROWS)[:, None] + jnp.arange(NA_WIN_ROWS)[None, :]
    t = rpb.astype(F32)[:, row_idx][:, :, :, col_idx]
    t = jnp.where(col_mask[None, None, None], t, NEG)
    t = t.transpose(0, 1, 3, 2, 4)
    return t.reshape(NA_HEADS, NA_WIN_ROWS, GRID_W, NA_WIN_ROWS * GRID_W)


def _rms(x, g):
    return x * lax.rsqrt(jnp.mean(x * x, axis=-1, keepdims=True) + RMS_EPS) * g


def _rope(x, cos, sin_signed):
    lane = lax.broadcasted_iota(jnp.int32, x.shape, 1)
    nxt = pltpu.roll(x, HEAD_DIM - 1, 1)
    prv = pltpu.roll(x, 1, 1)
    partner = jnp.where((lane & 1) == 0, nxt, prv)
    return x * cos + partner * sin_signed


def _gqa_kernel(q0_ref, q1_ref, q2_ref, k_ref, v_ref, kc_ref, vc_ref, cosk_ref, sink_ref,
                cosq_ref, sinq_ref, qn_ref, kn_ref, o_ref, ks, vs, *, S, L):
    @pl.when(pl.program_id(2) == 0)
    def _():
        k = _rope(_rms(k_ref[0].astype(F32), kn_ref[...]), cosk_ref[...], sink_ref[...])
        ks[0:S, :] = k.astype(BF16)
        ks[S:S + L, :] = _rms(kc_ref[0].astype(F32), kn_ref[...]).astype(BF16)
        vs[0:S, :] = v_ref[0]
        vs[S:S + L, :] = vc_ref[0]

    for g, q_ref in enumerate((q0_ref, q1_ref, q2_ref)):
        q = _rope(_rms(q_ref[0].astype(F32), qn_ref[...]), cosq_ref[...], sinq_ref[...]) * ATT_SCALE
        s = _nt_dot(q.astype(BF16), ks[...])
        m = jnp.max(s, axis=-1, keepdims=True)
        p = jnp.exp(s - m)
        l = jnp.sum(p, axis=-1, keepdims=True)
        o = jnp.dot(p.astype(BF16), vs[...], preferred_element_type=F32) / l
        o_ref[0, :, g * HEAD_DIM:(g + 1) * HEAD_DIM] = o.astype(o_ref.dtype)


def _gqa_attention(z3, zc3, cos, sin, qn, kn):
    B, S, _ = z3.shape
    L = zc3.shape[1]
    tq = _tile(S, 512)
    qspec = lambda g: pl.BlockSpec(
        (1, tq, HEAD_DIM), lambda b, h, i: (b, i, OFF_GQ_Q // HEAD_DIM + h * GQ_GROUP + g))
    kv = lambda n, off: pl.BlockSpec((1, n, HEAD_DIM), lambda b, h, i: (b, 0, off // HEAD_DIM + h))
    full = pl.BlockSpec((S, HEAD_DIM), lambda b, h, i: (0, 0))
    tile = pl.BlockSpec((tq, HEAD_DIM), lambda b, h, i: (i, 0))
    vec = pl.BlockSpec((1, HEAD_DIM), lambda b, h, i: (0, 0))
    return pl.pallas_call(
        functools.partial(_gqa_kernel, S=S, L=L),
        grid=(B, GQ_KV_HEADS, S // tq),
        in_specs=[qspec(0), qspec(1), qspec(2), kv(S, OFF_GQ_K), kv(S, OFF_GQ_V),
                  kv(L, OFF_GQ_K), kv(L, OFF_GQ_V), full, full, tile, tile, vec, vec],
        out_specs=pl.BlockSpec((1, tq, GQ_GROUP * HEAD_DIM), lambda b, h, i: (b, i, h)),
        out_shape=jax.ShapeDtypeStruct((B, S, GQ_W), BF16),
        scratch_shapes=[pltpu.VMEM((S + L, HEAD_DIM), BF16), pltpu.VMEM((S + L, HEAD_DIM), BF16)],
        compiler_params=_params("parallel", "parallel", "arbitrary"),
        name="gqa_attention",
    )(z3, z3, z3, z3, z3, zc3, zc3, cos, sin, cos, sin, qn, kn)


def _rope_tables(S):
    t = jnp.arange(S, dtype=jnp.int32)
    row = (t // GRID_W).astype(F32)
    col = (t % GRID_W).astype(F32)
    axis_dim = HEAD_DIM // 2
    inv_freq = 1.0 / (ROPE_THETA ** (jnp.arange(0, axis_dim, 2, dtype=F32) / axis_dim))
    ang = jnp.concatenate([row[:, None] * inv_freq, col[:, None] * inv_freq], axis=-1)
    cos = jnp.repeat(jnp.cos(ang), 2, axis=-1)
    sin = jnp.repeat(jnp.sin(ang), 2, axis=-1)
    sign = jnp.where(jnp.arange(HEAD_DIM) % 2 == 0, -1.0, 1.0).astype(F32)
    return cos, sin * sign


def _ctx_attn_kernel(q_ref, k_ref, v_ref, qn_ref, kn_ref, o_ref, *, norm):
    q = q_ref[0]
    k = k_ref[0]
    if norm:
        q = _rms(q.astype(F32), qn_ref[...]).astype(BF16)
        k = _rms(k.astype(F32), kn_ref[...]).astype(BF16)
    s = _nt_dot(q, k) * ATT_SCALE
    m = jnp.max(s, axis=-1, keepdims=True)
    p = jnp.exp(s - m)
    l = jnp.sum(p, axis=-1, keepdims=True)
    o_ref[0] = (jnp.dot(p.astype(BF16), v_ref[0], preferred_element_type=F32) / l).astype(o_ref.dtype)


def _ctx_attention(zc3, off_q, off_k, off_v, group, norm, qn, kn):
    B, L, _ = zc3.shape
    n_heads = NA_HEADS
    spec = lambda off, div: pl.BlockSpec((1, L, HEAD_DIM), lambda b, h: (b, 0, off // HEAD_DIM + h // div))
    vec = pl.BlockSpec((1, HEAD_DIM), lambda b, h: (0, 0))
    return pl.pallas_call(
        functools.partial(_ctx_attn_kernel, norm=norm),
        grid=(B, n_heads),
        in_specs=[spec(off_q, 1), spec(off_k, group), spec(off_v, group), vec, vec],
        out_specs=pl.BlockSpec((1, L, HEAD_DIM), lambda b, h: (b, 0, h)),
        out_shape=jax.ShapeDtypeStruct((B, L, n_heads * HEAD_DIM), BF16),
        compiler_params=_params("parallel", "parallel"),
        name="ctx_attention",
    )(zc3, zc3, zc3, qn, kn)


SSM_SLABS = SSM_WIDTH // LANES
SLAB_STATE = SSM_LANES // SSM_SLABS


def _s5_kernel(uf_ref, ub_ref, wb_ref, wc_ref, a_ref, h0_ref, yf_ref, yb_ref, hout_ref,
               br, bi, hs, *, tc, n, nb):
    j = pl.program_id(0)
    rows = tc * nb

    @pl.when(j == 0)
    def _():
        hs[...] = h0_ref[...]

    for d, (u_ref, y_ref) in enumerate(((uf_ref, yf_ref), (ub_ref, yb_ref))):
        u = u_ref[...].reshape(rows, SSM_WIDTH).astype(BF16)
        for k in range(SSM_SLABS):
            r = jnp.dot(u[:, k * LANES:(k + 1) * LANES], wb_ref[d, k], preferred_element_type=F32)
            br[:, k * SLAB_STATE:(k + 1) * SLAB_STATE] = r[:, :SLAB_STATE]
            bi[:, k * SLAB_STATE:(k + 1) * SLAB_STATE] = r[:, SLAB_STATE:]
        ar = jnp.broadcast_to(a_ref[d, 0], (nb, SSM_LANES))
        ai = jnp.broadcast_to(a_ref[d, 1], (nb, SSM_LANES))

        def step(i, carry, d=d, ar=ar, ai=ai):
            hr, hi = carry
            t = i if d == 0 else tc - 1 - i
            row = pl.multiple_of(t * nb, nb)
            nr = ar * hr - ai * hi + br[pl.ds(row, nb), :]
            ni = ar * hi + ai * hr + bi[pl.ds(row, nb), :]
            br[pl.ds(row, nb), :] = nr
            bi[pl.ds(row, nb), :] = ni
            return nr, ni

        hr, hi = lax.fori_loop(0, tc, step, (hs[2 * d], hs[2 * d + 1]))
        hs[2 * d] = hr
        hs[2 * d + 1] = hi
        for k in range(SSM_SLABS):
            sl = slice(k * SLAB_STATE, (k + 1) * SLAB_STATE)
            y = (jnp.dot(br[:, sl].astype(BF16), wc_ref[d, k, 0:SLAB_STATE, :], preferred_element_type=F32)
                 + jnp.dot(bi[:, sl].astype(BF16), wc_ref[d, k, SLAB_STATE:, :], preferred_element_type=F32))
            y_ref[:, :, k * LANES:(k + 1) * LANES] = y.reshape(tc, nb, LANES)

    @pl.when(j == n - 1)
    def _():
        hout_ref[...] = hs[...]


def _s5_scan(u_t, wb, wc, a, h0):
    T, nb, _ = u_t.shape
    tc = _tile(T, 64)
    n = T // tc
    ublk = lambda rev: pl.BlockSpec((tc, nb, SSM_WIDTH), (lambda j: (n - 1 - j, 0, 0)) if rev else (lambda j: (j, 0, 0)))
    const = lambda shape: pl.BlockSpec(shape, lambda j: (0,) * len(shape))
    return pl.pallas_call(
        functools.partial(_s5_kernel, tc=tc, n=n, nb=nb),
        grid=(n,),
        in_specs=[ublk(False), ublk(True), const(wb.shape), const(wc.shape), const(a.shape), const(h0.shape)],
        out_specs=[ublk(False), ublk(True), const(h0.shape)],
        out_shape=[jax.ShapeDtypeStruct((T, nb, SSM_WIDTH), F32), jax.ShapeDtypeStruct((T, nb, SSM_WIDTH), F32),
                   jax.ShapeDtypeStruct(h0.shape, F32)],
        scratch_shapes=[pltpu.VMEM((tc * nb, SSM_LANES), F32), pltpu.VMEM((tc * nb, SSM_LANES), F32),
                        pltpu.VMEM(h0.shape, F32)],
        compiler_params=_params("arbitrary"),
        name="s5_scan",
    )(u_t, u_t, wb, wc, a, h0)


def _s5_weights(lam_r, lam_i, log_dt, b_r, b_i, c_r, c_i):
    lam_r, lam_i, log_dt = lam_r.astype(F32), lam_i.astype(F32), log_dt.astype(F32)
    dt = jnp.exp(log_dt)[..., None]
    mag = jnp.exp(lam_r * dt)
    ar, ai = mag * jnp.cos(lam_i * dt), mag * jnp.sin(lam_i * dt)
    den = lam_r * lam_r + lam_i * lam_i
    nr, ni = ar - 1.0, ai
    kr, ki = (nr * lam_r + ni * lam_i) / den, (ni * lam_r - nr * lam_i) / den
    b_r, b_i = b_r.astype(F32), b_i.astype(F32)
    bbr = kr[..., None] * b_r - ki[..., None] * b_i
    bbi = kr[..., None] * b_i + ki[..., None] * b_r
    gs = SSM_GROUPS // SSM_SLABS
    eye = jnp.eye(gs, dtype=F32)

    def in_slab(w):
        w = w.reshape(2, SSM_SLABS, gs, SSM_STATE, SSM_GROUP)
        return jnp.einsum('dkgph,gj->dkghjp', w, eye).reshape(2, SSM_SLABS, gs * SSM_GROUP, gs * SSM_STATE)

    def out_slab(w):
        w = w.reshape(2, SSM_SLABS, gs, SSM_GROUP, SSM_STATE)
        return jnp.einsum('dkghp,gj->dkjpgh', w, eye).reshape(2, SSM_SLABS, gs * SSM_STATE, gs * SSM_GROUP)

    wb = jnp.concatenate([in_slab(bbr), in_slab(bbi)], axis=-1).astype(BF16)
    wc = jnp.concatenate([out_slab(c_r.astype(F32)), -out_slab(c_i.astype(F32))], axis=-2).astype(BF16)
    a = jnp.stack([ar.reshape(2, 1, SSM_LANES), ai.reshape(2, 1, SSM_LANES)], axis=1)
    return wb, wc, a


def _gelu_tanh(x):
    return 0.5 * x * (1.0 + jnp.tanh(math.sqrt(2.0 / math.pi) * (x + 0.044715 * (x * x * x))))


def _glu_kernel(u_ref, yf_ref, yb_ref, d_ref, w_ref, b_ref, o_ref):
    y = u_ref[0].astype(F32) * d_ref[...] + yf_ref[...] + yb_ref[...]
    t = _gelu_tanh(y).astype(BF16)
    r = jnp.dot(t, w_ref[...], preferred_element_type=F32) + b_ref[...]
    o_ref[0] = (r[:, :SSM_WIDTH] * _sigmoid(r[:, SSM_WIDTH:])).astype(o_ref.dtype)


def _glu(z3, yf, yb, d, w, b):
    B, S, _ = z3.shape
    ts = _tile(S, 512)
    y2 = lambda y: y.reshape(S, B * SSM_WIDTH)
    return pl.pallas_call(
        _glu_kernel,
        grid=(B, S // ts),
        in_specs=[pl.BlockSpec((1, ts, SSM_WIDTH), lambda b, i: (b, i, OFF_SU // SSM_WIDTH)),
                  pl.BlockSpec((ts, SSM_WIDTH), lambda b, i: (i, b)),
                  pl.BlockSpec((ts, SSM_WIDTH), lambda b, i: (i, b)),
                  pl.BlockSpec((1, SSM_WIDTH), lambda b, i: (0, 0)),
                  pl.BlockSpec((SSM_WIDTH, 2 * SSM_WIDTH), lambda b, i: (0, 0)),
                  pl.BlockSpec((1, 2 * SSM_WIDTH), lambda b, i: (0, 0))],
        out_specs=pl.BlockSpec((1, ts, SSM_WIDTH), lambda b, i: (b, i, 0)),
        out_shape=jax.ShapeDtypeStruct((B, S, SSM_WIDTH), BF16),
        compiler_params=_params("parallel", "parallel"),
        name="s5_glu",
    )(z3, y2(yf), y2(yb), d, w, b)


def _merge_kernel(a_ref, b_ref, s_ref, ga_ref, gb_ref, gs_ref, wa_ref, wb_ref, ws_ref, o_ref):
    m = (_sigmoid(ga_ref[...].astype(F32)) * jnp.dot(a_ref[...], wa_ref[...], preferred_element_type=F32)
         + _sigmoid(gb_ref[...].astype(F32)) * jnp.dot(b_ref[...], wb_ref[...], preferred_element_type=F32)
         + _sigmoid(gs_ref[...].astype(F32)) * jnp.dot(s_ref[...], ws_ref[...], preferred_element_type=F32))
    o_ref[...] = m.astype(o_ref.dtype)


def _merge(a, b, s, z2, wa, wb, ws):
    M = a.shape[0]
    D = wa.shape[1]
    tm, tn = _tile(M, 1024), _tile(D, 1024)
    row = lambda w: pl.BlockSpec((tm, w), lambda i, j: (i, 0))
    gate = lambda br: pl.BlockSpec((tm, tn), lambda i, j: (i, (OFF_GATE + br * D) // tn + j))
    wsp = lambda w: pl.BlockSpec((w, tn), lambda i, j: (0, j))
    return pl.pallas_call(
        _merge_kernel,
        grid=(M // tm, D // tn),
        in_specs=[row(NA_W), row(GQ_W), row(SSM_WIDTH), gate(0), gate(1), gate(2),
                  wsp(NA_W), wsp(GQ_W), wsp(SSM_WIDTH)],
        out_specs=pl.BlockSpec((tm, tn), lambda i, j: (i, j)),
        out_shape=jax.ShapeDtypeStruct((M, D), BF16),
        compiler_params=_params("parallel", "arbitrary"),
        name="merge",
    )(a, b, s, z2, z2, z2, wa, wb, ws)


def _layer_norm(x, g, b):
    mu = jnp.mean(x, axis=-1, keepdims=True)
    xc = x - mu
    var = jnp.mean(xc * xc, axis=-1, keepdims=True)
    return xc * lax.rsqrt(var + LN_EPS) * g + b


def _outproj_kernel(m_ref, w_ref, h_ref, g1_ref, sh_ref, sc_ref, lng_ref, lnb_ref, rwh_ref, rwl_ref, rb_ref,
                    cnt0_ref, hn_ref, tokr_ref, tokb_ref, idx_ref, wts_ref, rank_ref, cnt_ref, cnt, *, tm, D):
    @pl.when(pl.program_id(0) == 0)
    def _():
        cnt[...] = cnt0_ref[...]

    mix = jnp.dot(m_ref[...], w_ref[...], preferred_element_type=F32)
    hn = _layer_norm(DN_ALPHA * h_ref[...] + g1_ref[0] * mix, lng_ref[...], lnb_ref[...])
    hn_ref[...] = hn
    tok = hn * (1.0 + sc_ref[0]) + sh_ref[0]
    for r in range(D // LANES):
        tokr_ref[pl.ds(r, tm, stride=D // LANES), :] = tok[:, r * LANES:(r + 1) * LANES]
    t_hi = tok.astype(BF16)
    tokb_ref[...] = t_hi
    t_lo = (tok - t_hi.astype(F32)).astype(BF16)
    logits = (jnp.dot(t_hi, rwh_ref[...], preferred_element_type=F32)
              + jnp.dot(t_lo, rwh_ref[...], preferred_element_type=F32)
              + jnp.dot(t_hi, rwl_ref[...], preferred_element_type=F32))
    scores = _sigmoid(logits)
    sel = scores + rb_ref[...]
    lane = lax.broadcasted_iota(jnp.int32, sel.shape, 1).astype(F32)
    slot = lax.broadcasted_iota(jnp.int32, (tm, LANES), 1)
    idx_acc = jnp.zeros((tm, LANES), F32)
    w_acc = jnp.zeros((tm, LANES), F32)
    hits = []
    for k in range(TOP_K):
        mx = jnp.max(sel, axis=-1, keepdims=True)
        am = jnp.min(jnp.where(sel == mx, lane, float(N_EXPERTS)), axis=-1, keepdims=True)
        hit = lane == am
        hits.append(hit)
        wk = jnp.sum(jnp.where(hit, scores, 0.0), axis=-1, keepdims=True)
        idx_acc = jnp.where(slot == k, am, idx_acc)
        w_acc = jnp.where(slot == k, wk, w_acc)
        sel = jnp.where(hit, -jnp.inf, sel)
    wsum = jnp.sum(w_acc, axis=-1, keepdims=True)
    idx_ref[...] = idx_acc.astype(jnp.int32)
    wts_ref[...] = w_acc / wsum * ROUTED_SCALE

    picked = jnp.zeros(sel.shape, F32)
    for hit in hits:
        picked = jnp.where(hit, 1.0, picked)
    ri = lax.broadcasted_iota(jnp.int32, (tm, tm), 0)
    ci = lax.broadcasted_iota(jnp.int32, (tm, tm), 1)
    below = jnp.where(ci < ri, 1.0, 0.0).astype(BF16)
    before = jnp.dot(below, picked.astype(BF16), preferred_element_type=F32) + cnt[...]
    rank_acc = jnp.zeros((tm, LANES), F32)
    for k, hit in enumerate(hits):
        rk = jnp.sum(jnp.where(hit, before, 0.0), axis=-1, keepdims=True)
        rank_acc = jnp.where(slot == k, rk, rank_acc)
    rank_ref[...] = rank_acc.astype(jnp.int32)
    cnt[...] = cnt[...] + jnp.sum(picked, axis=0, keepdims=True)
    cnt_ref[...] = cnt[...]


def _outproj_ln_route(m, w_out, h2, mod3, row_fn, ln_g, ln_b, rw_hi, rw_lo, rbias, cnt0, tm=256):
    M, D = m.shape
    tm = _tile(M, tm)
    nr = D // LANES
    vec = lambda k: pl.BlockSpec((1, 1, D), lambda i: (row_fn(i), 0, k))
    const = lambda shape: pl.BlockSpec(shape, lambda i: (0,) * len(shape))
    rowblk = lambda w: pl.BlockSpec((tm, w), lambda i: (i, 0))
    return pl.pallas_call(
        functools.partial(_outproj_kernel, tm=tm, D=D),
        grid=(M // tm,),
        in_specs=[rowblk(D), const((D, D)), rowblk(D), vec(2), vec(3), vec(4), const((1, D)), const((1, D)),
                  const((D, N_EXPERTS)), const((D, N_EXPERTS)), const((1, N_EXPERTS)), const((1, N_EXPERTS))],
        out_specs=[rowblk(D), pl.BlockSpec((tm * nr, LANES), lambda i: (i, 0)), rowblk(D),
                   rowblk(LANES), rowblk(LANES), rowblk(LANES), const((1, N_EXPERTS))],
        out_shape=[jax.ShapeDtypeStruct((M, D), F32), jax.ShapeDtypeStruct((M * nr, LANES), F32),
                   jax.ShapeDtypeStruct((M, D), BF16), jax.ShapeDtypeStruct((M, LANES), jnp.int32),
                   jax.ShapeDtypeStruct((M, LANES), F32), jax.ShapeDtypeStruct((M, LANES), jnp.int32),
                   jax.ShapeDtypeStruct((1, N_EXPERTS), F32)],
        scratch_shapes=[pltpu.VMEM((1, N_EXPERTS), F32)],
        compiler_params=_params("arbitrary"),
        name="outproj_ln_route",
    )(m, w_out, h2, mod3, mod3, mod3, ln_g, ln_b, rw_hi, rw_lo, rbias, cnt0)


def _swiglu_kernel(x_ref, w13_ref, w2_ref, o_ref, *, E):
    h = jnp.dot(x_ref[...], w13_ref[...], preferred_element_type=F32)
    g = h[:, :E]
    act = (g * _sigmoid(g) * h[:, E:]).astype(BF16)
    o_ref[...] = jnp.dot(act, w2_ref[...], preferred_element_type=F32).astype(o_ref.dtype)


def _shared_expert(tokb, w13, w2):
    T, D = tokb.shape
    E = w2.shape[0]
    tm = _tile(T, 512)
    return pl.pallas_call(
        functools.partial(_swiglu_kernel, E=E),
        grid=(T // tm,),
        in_specs=[pl.BlockSpec((tm, D), lambda i: (i, 0)), pl.BlockSpec((D, 2 * E), lambda i: (0, 0)),
                  pl.BlockSpec((E, D), lambda i: (0, 0))],
        out_specs=pl.BlockSpec((tm, D), lambda i: (i, 0)),
        out_shape=jax.ShapeDtypeStruct((T, D), BF16),
        compiler_params=_params("parallel"),
        name="shared_expert",
    )(tokb, w13, w2)


def _route_plan(idx, rank, counts, tm):
    T = idx.shape[0]
    counts = counts.reshape(N_EXPERTS).astype(jnp.int32)
    padded = (counts + tm - 1) // tm * tm
    pend = jnp.cumsum(padded)
    pstart = pend - padded
    dest = (pstart[idx] + rank).astype(jnp.int32)
    n_blk = -(-(T * TOP_K + N_EXPERTS * (tm - 1)) // tm)
    blk_exp = jnp.minimum(jnp.searchsorted(pend, jnp.arange(n_blk, dtype=jnp.int32) * tm, side='right'),
                          N_EXPERTS - 1).astype(jnp.int32)
    n_used = (pend[-1:] // tm).astype(jnp.int32)
    return dest, blk_exp, n_used, (pstart + counts).astype(jnp.int32), (padded - counts).astype(jnp.int32)


def _dispatch_kernel(ps_ref, pc_ref, nu_ref, dest_ref, tok_ref, xs_hbm, zbuf, sem, *, tb, nr, n, tm, n_blk):
    i = pl.program_id(0)

    def pad_rows(do):
        def per_expert(e, c):
            def per_row(r, c2):
                row = pl.multiple_of((ps_ref[e] + r) * nr, nr)
                do(pltpu.make_async_copy(zbuf.at[pl.ds(0, nr)], xs_hbm.at[pl.ds(row, nr)], sem.at[1]))
                return c2
            return lax.fori_loop(0, pc_ref[e], per_row, c)
        lax.fori_loop(0, N_EXPERTS, per_expert, 0)

        def per_block(b, c):
            row = pl.multiple_of(b * (tm * nr), tm * nr)
            do(pltpu.make_async_copy(zbuf, xs_hbm.at[pl.ds(row, tm * nr)], sem.at[1]))
            return c
        lax.fori_loop(nu_ref[0], n_blk, per_block, 0)

    @pl.when(i == 0)
    def _():
        zbuf[...] = jnp.zeros(zbuf.shape, zbuf.dtype)
        pad_rows(lambda cp: cp.start())

    def body(t, c):
        src = tok_ref.at[pl.ds(pl.multiple_of(t * nr, nr), nr)]
        for k in range(TOP_K):
            d = dest_ref[0, 0, t * TOP_K + k]
            pltpu.make_async_copy(src, xs_hbm.at[pl.ds(pl.multiple_of(d * nr, nr), nr)], sem.at[0]).start()
        return c
    lax.fori_loop(0, tb, body, 0)
    for k in range(TOP_K):
        pltpu.make_async_copy(tok_ref, xs_hbm.at[pl.ds(0, tb * nr)], sem.at[0]).wait()

    @pl.when(i == n - 1)
    def _():
        pad_rows(lambda cp: cp.wait())


def _dispatch(tokr, dest, pad_start, pad_cnt, n_used, n_blk, tm, tb=256):
    T = dest.shape[0]
    nr = tokr.shape[0] // T
    tb = _tile(T, tb)
    n = T // tb
    grid_spec = pltpu.PrefetchScalarGridSpec(
        num_scalar_prefetch=3,
        grid=(n,),
        in_specs=[pl.BlockSpec((1, 1, tb * TOP_K), lambda i, ps, pc, nu: (i, 0, 0), memory_space=pltpu.SMEM),
                  pl.BlockSpec((tb * nr, LANES), lambda i, ps, pc, nu: (i, 0))],
        out_specs=pl.BlockSpec(memory_space=pl.ANY),
        scratch_shapes=[pltpu.VMEM((tm * nr, LANES), F32), pltpu.SemaphoreType.DMA((2,))],
    )
    return pl.pallas_call(
        functools.partial(_dispatch_kernel, tb=tb, nr=nr, n=n, tm=tm, n_blk=n_blk),
        grid_spec=grid_spec,
        out_shape=jax.ShapeDtypeStruct((n_blk * tm * nr, LANES), F32),
        compiler_params=pltpu.CompilerParams(dimension_semantics=("arbitrary",), vmem_limit_bytes=VMEM_LIMIT,
                                             disable_bounds_checks=True),
        name="moe_dispatch",
    )(pad_start, pad_cnt, n_used, dest.reshape(n, 1, tb * TOP_K), tokr)


def _expert_kernel(be_ref, nu_ref, x_ref, w1_ref, w3_ref, w2_ref, y_ref, w13s, w2s, *, tm, D, E):
    i = pl.program_id(0)
    nr = D // LANES

    @pl.when((i == 0) | (be_ref[i] != be_ref[jnp.maximum(i - 1, 0)]))
    def _():
        w13s[:, 0:E] = w1_ref[...].astype(BF16)
        w13s[:, E:2 * E] = w3_ref[...].astype(BF16)
        w2s[...] = w2_ref[...].astype(BF16)

    @pl.when(i < nu_ref[0])
    def _():
        x = jnp.concatenate([x_ref[pl.ds(r, tm, stride=nr), :] for r in range(nr)], axis=1).astype(BF16)
        h = jnp.dot(x, w13s[...], preferred_element_type=F32)
        g = h[:, :E]
        act = (g * _sigmoid(g) * h[:, E:]).astype(BF16)
        y = jnp.dot(act, w2s[...], preferred_element_type=F32)
        for r in range(nr):
            y_ref[pl.ds(r, tm, stride=nr), :] = y[:, r * LANES:(r + 1) * LANES]

    @pl.when(i >= nu_ref[0])
    def _():
        y_ref[...] = jnp.zeros(y_ref.shape, y_ref.dtype)


def _routed_experts(xs, blk_exp, n_used, w1, w3, w2, l, tm):
    n = blk_exp.shape[0]
    _, _, D, E = w1.shape
    nr = D // LANES
    wspec = lambda a, b: pl.BlockSpec((None, None, a, b), lambda i, be, nu: (l, be[i], 0, 0))
    grid_spec = pltpu.PrefetchScalarGridSpec(
        num_scalar_prefetch=2,
        grid=(n,),
        in_specs=[pl.BlockSpec((tm * nr, LANES), lambda i, be, nu: (jnp.where(i < nu[0], i, 0), 0)),
                  wspec(D, E), wspec(D, E), wspec(E, D)],
        out_specs=pl.BlockSpec((tm * nr, LANES), lambda i, be, nu: (i, 0)),
        scratch_shapes=[pltpu.VMEM((D, 2 * E), BF16), pltpu.VMEM((E, D), BF16)],
    )
    return pl.pallas_call(
        functools.partial(_expert_kernel, tm=tm, D=D, E=E),
        grid_spec=grid_spec,
        out_shape=jax.ShapeDtypeStruct((n * tm * nr, LANES), F32),
        compiler_params=_params("arbitrary"),
        name="routed_experts",
    )(blk_exp, n_used, xs, w1, w3, w2)


def _combine_kernel(dcur_ref, dnxt_ref, h_ref, sh_ref, w_ref, g2_ref, lng_ref, lnb_ref, ys_hbm, o_ref,
                    ybuf, sem, *, tb, D, n):
    i = pl.program_id(0)
    nr = D // LANES
    rows = tb * TOP_K * nr

    def gather(slot, d_ref):
        def body(t, c):
            for k in range(TOP_K):
                a = t * TOP_K + k
                d = d_ref[0, 0, a]
                pltpu.make_async_copy(ys_hbm.at[pl.ds(pl.multiple_of(d * nr, nr), nr)],
                                      ybuf.at[pl.ds(pl.multiple_of(slot * rows + a * nr, nr), nr)],
                                      sem.at[slot]).start()
            return c
        lax.fori_loop(0, tb, body, 0)

    @pl.when(i == 0)
    def _():
        gather(0, dcur_ref)

    @pl.when(i + 1 < n)
    def _():
        gather((i + 1) % 2, dnxt_ref)

    slot = i % 2
    base = slot * rows
    pltpu.make_async_copy(ys_hbm.at[pl.ds(0, rows)], ybuf.at[pl.ds(pl.multiple_of(base, rows), rows)],
                          sem.at[slot]).wait()
    w = w_ref[...]
    wk = [jnp.broadcast_to(w[:, k:k + 1], (tb, LANES)) for k in range(TOP_K)]
    parts = []
    for r in range(nr):
        acc = wk[0] * ybuf[pl.ds(base + r, tb, stride=TOP_K * nr), :]
        for k in range(1, TOP_K):
            acc = acc + wk[k] * ybuf[pl.ds(base + k * nr + r, tb, stride=TOP_K * nr), :]
        parts.append(acc)
    f = jnp.concatenate(parts, axis=1) + sh_ref[...].astype(F32)
    o_ref[...] = _layer_norm(DN_ALPHA * h_ref[...] + g2_ref[0] * f, lng_ref[...], lnb_ref[...])


def _combine_ln(h2, shared, wts, ys, dest, tok_off, mod3, row_fn, ln_g, ln_b, tb=128):
    M, D = h2.shape
    tb = _tile(M, tb)
    nr = D // LANES
    n = M // tb
    off = tok_off // tb
    dest3 = dest.reshape(-1, 1, tb * TOP_K)
    rowblk = lambda w, o: pl.BlockSpec((tb, w), lambda i: (i + o, 0))
    const = lambda shape: pl.BlockSpec(shape, lambda i: (0,) * len(shape))
    smem = lambda fn: pl.BlockSpec((1, 1, tb * TOP_K), fn, memory_space=pltpu.SMEM)
    return pl.pallas_call(
        functools.partial(_combine_kernel, tb=tb, D=D, n=n),
        grid=(n,),
        in_specs=[smem(lambda i: (i + off, 0, 0)), smem(lambda i: (jnp.minimum(i + 1, n - 1) + off, 0, 0)),
                  rowblk(D, 0), rowblk(D, off), rowblk(LANES, off),
                  pl.BlockSpec((1, 1, D), lambda i: (row_fn(i), 0, 5)), const((1, D)), const((1, D)),
                  pl.BlockSpec(memory_space=pl.ANY)],
        out_specs=rowblk(D, 0),
        out_shape=jax.ShapeDtypeStruct((M, D), F32),
        scratch_shapes=[pltpu.VMEM((2 * tb * TOP_K * nr, LANES), F32), pltpu.SemaphoreType.DMA((2,))],
        compiler_params=pltpu.CompilerParams(dimension_semantics=("arbitrary",), vmem_limit_bytes=VMEM_LIMIT,
                                             disable_bounds_checks=True),
        name="combine_ln",
    )(dest3, dest3, h2, shared, wts, mod3, ln_g, ln_b, ys)


MOE_TM = 256


def kernel(x, c, ctx, c_ctx, w_mod, b_mod, w_in, na_rpb, gq_q_norm, gq_k_norm, ssm_lam_re, ssm_lam_im, ssm_log_dt, ssm_b_re, ssm_b_im, ssm_c_re, ssm_c_im, ssm_d, ssm_glu_w, ssm_glu_b, w_br_na, w_br_gq, w_br_ssm, w_out, ln1_g, ln1_b, router_w, router_bias, exp_w1, exp_w3, exp_w2, sh_w1, sh_w3, sh_w2, ln2_g, ln2_b):
    B, S, D = x.shape
    L = ctx.shape[1]
    assert B < 2 * SUBLANES and D % LANES == 0
    cos, sin = _rope_tables(S)
    c_all = jnp.zeros((2 * SUBLANES, D), F32).at[:B].set(c).at[B].set(c_ctx)
    lat_row = lambda b: b
    ctx_row = lambda b: B
    h, hc = x, ctx
    for l in range(DEPTH):
        need_ctx = l < DEPTH - 1
        mod3 = _modvec(c_all, w_mod, l, b_mod[l])[:, None, :]
        vec = lambda p: p[l].reshape(1, -1).astype(F32)

        w_in_b = w_in[l].astype(BF16)
        u = _modulate(h, mod3, lat_row, 0, 1)
        uc = _modulate(hc, mod3, ctx_row, 0, 1)
        z2 = _matmul(u.reshape(B * S, D), w_in_b, BF16)
        zc2 = _matmul(uc.reshape(B * L, D), w_in_b if need_ctx else w_in_b[:, :CTX_IN_W], BF16)
        z3, zc3 = z2.reshape(B, S, -1), zc2.reshape(B, L, -1)

        a_out = _na_attention(z3, zc3, _na_bias_table(na_rpb[l]))
        qn, kn = vec(gq_q_norm), vec(gq_k_norm)
        b_out = _gqa_attention(z3, zc3, cos, sin, qn, kn)

        wb, wc, a = _s5_weights(ssm_lam_re[l], ssm_lam_im[l], ssm_log_dt[l], ssm_b_re[l], ssm_b_im[l],
                                ssm_c_re[l], ssm_c_im[l])
        su = lambda t: t[:, :, OFF_SU:OFF_SU + SSM_WIDTH].astype(F32).transpose(1, 0, 2)
        h0 = jnp.zeros((4, B, SSM_LANES), F32)
        ycf, ycb, hfin = _s5_scan(su(zc3), wb, wc, a, h0)
        yf, yb, _ = _s5_scan(su(z3), wb, wc, a, hfin)
        glu_w, glu_b, dvec = ssm_glu_w[l].astype(BF16), vec(ssm_glu_b), vec(ssm_d)
        c_out = _glu(z3, yf, yb, dvec, glu_w, glu_b)

        wa, wg, ws, wo = (w[l].astype(BF16) for w in (w_br_na, w_br_gq, w_br_ssm, w_out))
        m = _merge(a_out.reshape(B * S, -1), b_out.reshape(B * S, -1), c_out.reshape(B * S, -1), z2, wa, wg, ws)
        rw = router_w[l].astype(F32)
        rw_hi = rw.astype(BF16)
        rw_lo = (rw - rw_hi.astype(F32)).astype(BF16)
        rbias = vec(router_bias)
        g1, b1 = vec(ln1_g), vec(ln1_b)
        tm_o = _tile(S, 256)
        h2, tokr, tokb, idx, wts, rank, counts = _outproj_ln_route(
            m, wo, h.reshape(B * S, D), mod3, lambda i: i // (S // tm_o), g1, b1, rw_hi, rw_lo, rbias,
            jnp.zeros((1, N_EXPERTS), F32), tm=tm_o)
        if need_ctx:
            a_ctx = _ctx_attention(zc3, OFF_NA_Q, OFF_NA_K, OFF_NA_V, 1, False, qn, kn)
            b_ctx = _ctx_attention(zc3, OFF_GQ_Q, OFF_GQ_K, OFF_GQ_V, GQ_GROUP, True, qn, kn)
            c_ctx_out = _glu(zc3, ycf, ycb, dvec, glu_w, glu_b)
            mc = _merge(a_ctx.reshape(B * L, -1), b_ctx.reshape(B * L, -1), c_ctx_out.reshape(B * L, -1),
                        zc2, wa, wg, ws)
            tm_c = _tile(L, 256)
            hc2, tokr_c, tokb_c, idx_c, wts_c, rank_c, counts = _outproj_ln_route(
                mc, wo, hc.reshape(B * L, D), mod3, lambda i: B, g1, b1, rw_hi, rw_lo, rbias, counts, tm=tm_c)
            tokr = jnp.concatenate([tokr, tokr_c], axis=0)
            tokb = jnp.concatenate([tokb, tokb_c], axis=0)
            idx = jnp.concatenate([idx, idx_c], axis=0)
            wts = jnp.concatenate([wts, wts_c], axis=0)
            rank = jnp.concatenate([rank, rank_c], axis=0)

        w13 = jnp.concatenate([sh_w1[l], sh_w3[l]], axis=1).astype(BF16)
        shared = _shared_expert(tokb, w13, sh_w2[l].astype(BF16))
        dest, blk_exp, n_used, pad_start, pad_cnt = _route_plan(idx[:, :TOP_K], rank[:, :TOP_K], counts, MOE_TM)
        xs = _dispatch(tokr, dest, pad_start, pad_cnt, n_used, blk_exp.shape[0], MOE_TM)
        ys = _routed_experts(xs, blk_exp, n_used, exp_w1, exp_w3, exp_w2, l, MOE_TM)
        g2, b2 = vec(ln2_g), vec(ln2_b)
        tb_l = _tile(S, 128)
        h = _combine_ln(h2, shared, wts, ys, dest, 0, mod3, lambda i: i // (S // tb_l), g2, b2,
                        tb=tb_l).reshape(B, S, D)
        if need_ctx:
            hc = _combine_ln(hc2, shared, wts, ys, dest, B * S, mod3, lambda i: B, g2, b2,
                             tb=_tile(L, 128)).reshape(B, L, D)
    return h
```

```python
import functools
import math

import jax
import jax.numpy as jnp
import numpy as np
from jax import lax
from jax.experimental import pallas as pl
from jax.experimental.pallas import tpu as pltpu

F32 = jnp.float32
BF16 = jnp.bfloat16

DEPTH = 2
GRID_W = 64
HEAD_DIM = 128
NA_HEADS = 6
NA_WIN_ROWS = 8
NA_WIN_COLS = 16
GQ_HEADS = 6
GQ_KV_HEADS = 2
GQ_GROUP = GQ_HEADS // GQ_KV_HEADS
ROPE_THETA = 10000.0
SSM_GROUP = 16
SSM_WIDTH = 512
SSM_GROUPS = SSM_WIDTH // SSM_GROUP
SSM_STATE = 64
SSM_LANES = SSM_GROUPS * SSM_STATE
N_EXPERTS = 64
TOP_K = 8
EXPERT_DIM = 512
ROUTED_SCALE = 2.5
DN_ALPHA = (2 * DEPTH) ** 0.25
LN_EPS = 1e-6
RMS_EPS = 1e-6
ATT_SCALE = HEAD_DIM ** -0.5
NEG = -1e30
NA_QROWS = 8
NA_KROWS = 16

LANES = 128
SUBLANES = 8
VMEM_LIMIT = 56 * 1024 * 1024

NA_W = NA_HEADS * HEAD_DIM
GQ_W = GQ_HEADS * HEAD_DIM
GQ_KV_W = GQ_KV_HEADS * HEAD_DIM
OFF_NA_K = 0
OFF_NA_V = OFF_NA_K + NA_W
OFF_GQ_K = OFF_NA_V + NA_W
OFF_GQ_V = OFF_GQ_K + GQ_KV_W
OFF_SU = OFF_GQ_V + GQ_KV_W
CTX_IN_W = OFF_SU + SSM_WIDTH
OFF_NA_Q = CTX_IN_W
OFF_GQ_Q = OFF_NA_Q + NA_W
OFF_GATE = OFF_GQ_Q + GQ_W


def _tile(n, pref):
    t = min(n, pref)
    while n % t:
        t //= 2
    return t


def _params(*sem):
    return pltpu.CompilerParams(dimension_semantics=sem, vmem_limit_bytes=VMEM_LIMIT)


def _sigmoid(x):
    return 1.0 / (1.0 + jnp.exp(-x))


def _modvec_kernel(c_ref, w_ref, b_ref, o_ref):
    c = c_ref[...]
    a = (c * _sigmoid(c)).astype(BF16)
    o_ref[...] = jnp.dot(a, w_ref[...].astype(BF16), preferred_element_type=F32) + b_ref[...]


def _modvec(c_all, w, l, b):
    R, D = c_all.shape
    N = w.shape[2]
    tn = _tile(N, 1024)
    return pl.pallas_call(
        _modvec_kernel,
        grid=(N // tn,),
        in_specs=[pl.BlockSpec((R, D), lambda j: (0, 0)),
                  pl.BlockSpec((None, D, tn), lambda j: (l, 0, j)),
                  pl.BlockSpec((1, tn), lambda j: (0, j))],
        out_specs=pl.BlockSpec((R, tn), lambda j: (0, j)),
        out_shape=jax.ShapeDtypeStruct((R, N), F32),
        compiler_params=_params("arbitrary"),
        name="modvec",
    )(c_all, w, b.reshape(1, N))


def _modulate_kernel(h_ref, sh_ref, sc_ref, o_ref):
    o_ref[0] = (h_ref[0] * (1.0 + sc_ref[0]) + sh_ref[0]).astype(o_ref.dtype)


def _modulate(h, mod3, row_fn, k_shift, k_scale):
    B, S, D = h.shape
    ts = _tile(S, 512)
    return pl.pallas_call(
        _modulate_kernel,
        grid=(B, S // ts),
        in_specs=[pl.BlockSpec((1, ts, D), lambda b, i: (b, i, 0)),
                  pl.BlockSpec((1, 1, D), lambda b, i: (row_fn(b), 0, k_shift)),
                  pl.BlockSpec((1, 1, D), lambda b, i: (row_fn(b), 0, k_scale))],
        out_specs=pl.BlockSpec((1, ts, D), lambda b, i: (b, i, 0)),
        out_shape=jax.ShapeDtypeStruct((B, S, D), BF16),
        compiler_params=_params("parallel", "parallel"),
        name="modulate",
    )(h, mod3, mod3)


def _mm_kernel(x_ref, w_ref, o_ref):
    o_ref[...] = jnp.dot(x_ref[...], w_ref[...], preferred_element_type=F32).astype(o_ref.dtype)


def _matmul(x, w, out_dtype, tm=1024, tn=1024):
    M, K = x.shape
    N = w.shape[1]
    tm, tn = _tile(M, tm), _tile(N, tn)
    return pl.pallas_call(
        _mm_kernel,
        grid=(M // tm, N // tn),
        in_specs=[pl.BlockSpec((tm, K), lambda i, j: (i, 0)),
                  pl.BlockSpec((K, tn), lambda i, j: (0, j))],
        out_specs=pl.BlockSpec((tm, tn), lambda i, j: (i, j)),
        out_shape=jax.ShapeDtypeStruct((M, N), out_dtype),
        compiler_params=_params("parallel", "arbitrary"),
        name="matmul",
    )(x, w)


def _nt_dot(a, b):
    return lax.dot_general(a, b, (((1,), (1,)), ((), ())), preferred_element_type=F32)


def _na_key_row0(blk, rows):
    return min(max(NA_QROWS * blk - NA_WIN_ROWS // 2, 0), rows - NA_KROWS)


def _na_kernel(q_ref, k_ref, v_ref, kc_ref, vc_ref, tab_ref, o_ref, *, rows):
    kc = kc_ref[0]
    vc = vc_ref[0]
    nq = NA_QROWS * GRID_W
    for blk in range(rows // NA_QROWS):
        k0 = _na_key_row0(blk, rows) * GRID_W
        q = q_ref[0, blk * nq:(blk + 1) * nq, :]
        kb = k_ref[0, k0:k0 + NA_KROWS * GRID_W, :]
        vb = v_ref[0, k0:k0 + NA_KROWS * GRID_W, :]
        s1 = _nt_dot(q, kb) * ATT_SCALE + tab_ref[0, blk]
        s2 = _nt_dot(q, kc) * ATT_SCALE
        m = jnp.maximum(jnp.max(s1, axis=-1, keepdims=True), jnp.max(s2, axis=-1, keepdims=True))
        p1 = jnp.exp(s1 - m)
        p2 = jnp.exp(s2 - m)
        l = jnp.sum(p1, axis=-1, keepdims=True) + jnp.sum(p2, axis=-1, keepdims=True)
        o = (jnp.dot(p1.astype(BF16), vb, preferred_element_type=F32)
             + jnp.dot(p2.astype(BF16), vc, preferred_element_type=F32))
        o_ref[0, blk * nq:(blk + 1) * nq, :] = (o / l).astype(o_ref.dtype)


def _na_attention(z3, zc3, tab):
    B, S, _ = z3.shape
    L = zc3.shape[1]
    rows = S // GRID_W
    assert rows >= NA_KROWS and rows % NA_QROWS == 0
    blk = lambda n, off: pl.BlockSpec((1, n, HEAD_DIM), lambda h, b: (b, 0, off // HEAD_DIM + h))
    return pl.pallas_call(
        functools.partial(_na_kernel, rows=rows),
        grid=(NA_HEADS, B),
        in_specs=[blk(S, OFF_NA_Q), blk(S, OFF_NA_K), blk(S, OFF_NA_V),
                  blk(L, OFF_NA_K), blk(L, OFF_NA_V),
                  pl.BlockSpec((1,) + tab.shape[1:], lambda h, b: (h, 0, 0, 0))],
        out_specs=pl.BlockSpec((1, S, HEAD_DIM), lambda h, b: (b, 0, h)),
        out_shape=jax.ShapeDtypeStruct((B, S, NA_W), BF16),
        compiler_params=_params("parallel", "parallel"),
        name="na_attention",
    )(z3, z3, z3, zc3, zc3, tab)


def _na_bias_table(rpb, rows):
    col = jnp.arange(GRID_W, dtype=jnp.int32)
    col_start = jnp.clip(col - NA_WIN_COLS // 2, 0, GRID_W - NA_WIN_COLS)
    col_mask = (col[None, :] >= col_start[:, None]) & (col[None, :] < col_start[:, None] + NA_WIN_COLS)
    col_idx = jnp.clip(col[None, :] - col[:, None], -(NA_WIN_COLS - 1), NA_WIN_COLS - 1) + NA_WIN_COLS - 1
    nblk = rows // NA_QROWS
    qrow = NA_QROWS * jnp.arange(nblk)[:, None] + jnp.arange(NA_QROWS)[None, :]
    krow = jnp.array([_na_key_row0(b, rows) for b in range(nblk)])[:, None] + jnp.arange(NA_KROWS)[None, :]
    win0 = jnp.clip(qrow - NA_WIN_ROWS // 2, 0, rows - NA_WIN_ROWS)[:, :, None]
    krow = krow[:, None, :]
    row_ok = (krow >= win0) & (krow < win0 + NA_WIN_ROWS)
    row_idx = jnp.clip(krow - qrow[:, :, None] + NA_WIN_ROWS - 1, 0, 2 * NA_WIN_ROWS - 2)
    t = rpb.astype(F32)[:, row_idx][..., col_idx]
    t = jnp.where(row_ok[None, :, :, :, None, None] & col_mask[None, None, None, None], t, NEG)
    t = t.transpose(0, 1, 2, 4, 3, 5)
    return t.reshape(NA_HEADS, nblk, NA_QROWS * GRID_W, NA_KROWS * GRID_W)


def _rms(x, g):
    return x * lax.rsqrt(jnp.mean(x * x, axis=-1, keepdims=True) + RMS_EPS) * g


def _rope(x, cos, sin_signed):
    lane = lax.broadcasted_iota(jnp.int32, x.shape, 1)
    nxt = pltpu.roll(x, HEAD_DIM - 1, 1)
    prv = pltpu.roll(x, 1, 1)
    partner = jnp.where((lane & 1) == 0, nxt, prv)
    return x * cos + partner * sin_signed


def _gqa_kernel(q0_ref, q1_ref, q2_ref, k_ref, v_ref, kc_ref, vc_ref, cosk_ref, sink_ref,
                cosq_ref, sinq_ref, qn_ref, kn_ref, o_ref, ks, vs, *, S, L):
    @pl.when(pl.program_id(2) == 0)
    def _():
        k = _rope(_rms(k_ref[0].astype(F32), kn_ref[...]), cosk_ref[...], sink_ref[...])
        ks[0:S, :] = k.astype(BF16)
        ks[S:S + L, :] = _rms(kc_ref[0].astype(F32), kn_ref[...]).astype(BF16)
        vs[0:S, :] = v_ref[0]
        vs[S:S + L, :] = vc_ref[0]

    for g, q_ref in enumerate((q0_ref, q1_ref, q2_ref)):
        q = _rope(_rms(q_ref[0].astype(F32), qn_ref[...]), cosq_ref[...], sinq_ref[...]) * ATT_SCALE
        s = _nt_dot(q.astype(BF16), ks[...])
        m = jnp.max(s, axis=-1, keepdims=True)
        p = jnp.exp(s - m)
        l = jnp.sum(p, axis=-1, keepdims=True)
        o = jnp.dot(p.astype(BF16), vs[...], preferred_element_type=F32) / l
        o_ref[0, :, g * HEAD_DIM:(g + 1) * HEAD_DIM] = o.astype(o_ref.dtype)


def _gqa_attention(z3, zc3, cos, sin, qn, kn):
    B, S, _ = z3.shape
    L = zc3.shape[1]
    tq = _tile(S, 512)
    qspec = lambda g: pl.BlockSpec(
        (1, tq, HEAD_DIM), lambda b, h, i: (b, i, OFF_GQ_Q // HEAD_DIM + h * GQ_GROUP + g))
    kv = lambda n, off: pl.BlockSpec((1, n, HEAD_DIM), lambda b, h, i: (b, 0, off // HEAD_DIM + h))
    full = pl.BlockSpec((S, HEAD_DIM), lambda b, h, i: (0, 0))
    tile = pl.BlockSpec((tq, HEAD_DIM), lambda b, h, i: (i, 0))
    vec = pl.BlockSpec((1, HEAD_DIM), lambda b, h, i: (0, 0))
    return pl.pallas_call(
        functools.partial(_gqa_kernel, S=S, L=L),
        grid=(B, GQ_KV_HEADS, S // tq),
        in_specs=[qspec(0), qspec(1), qspec(2), kv(S, OFF_GQ_K), kv(S, OFF_GQ_V),
                  kv(L, OFF_GQ_K), kv(L, OFF_GQ_V), full, full, tile, tile, vec, vec],
        out_specs=pl.BlockSpec((1, tq, GQ_GROUP * HEAD_DIM), lambda b, h, i: (b, i, h)),
        out_shape=jax.ShapeDtypeStruct((B, S, GQ_W), BF16),
        scratch_shapes=[pltpu.VMEM((S + L, HEAD_DIM), BF16), pltpu.VMEM((S + L, HEAD_DIM), BF16)],
        compiler_params=_params("parallel", "parallel", "arbitrary"),
        name="gqa_attention",
    )(z3, z3, z3, z3, z3, zc3, zc3, cos, sin, cos, sin, qn, kn)


def _rope_tables(S):
    t = jnp.arange(S, dtype=jnp.int32)
    row = (t // GRID_W).astype(F32)
    col = (t % GRID_W).astype(F32)
    axis_dim = HEAD_DIM // 2
    inv_freq = 1.0 / (ROPE_THETA ** (jnp.arange(0, axis_dim, 2, dtype=F32) / axis_dim))
    ang = jnp.concatenate([row[:, None] * inv_freq, col[:, None] * inv_freq], axis=-1)
    cos = jnp.repeat(jnp.cos(ang), 2, axis=-1)
    sin = jnp.repeat(jnp.sin(ang), 2, axis=-1)
    sign = jnp.where(jnp.arange(HEAD_DIM) % 2 == 0, -1.0, 1.0).astype(F32)
    return cos, sin * sign


def _ctx_attn_kernel(q_ref, k_ref, v_ref, qn_ref, kn_ref, o_ref, *, norm):
    q = q_ref[0]
    k = k_ref[0]
    if norm:
        q = _rms(q.astype(F32), qn_ref[...]).astype(BF16)
        k = _rms(k.astype(F32), kn_ref[...]).astype(BF16)
    s = _nt_dot(q, k) * ATT_SCALE
    m = jnp.max(s, axis=-1, keepdims=True)
    p = jnp.exp(s - m)
    l = jnp.sum(p, axis=-1, keepdims=True)
    o_ref[0] = (jnp.dot(p.astype(BF16), v_ref[0], preferred_element_type=F32) / l).astype(o_ref.dtype)


def _ctx_attention(zc3, off_q, off_k, off_v, group, norm, qn, kn):
    B, L, _ = zc3.shape
    n_heads = NA_HEADS
    spec = lambda off, div: pl.BlockSpec((1, L, HEAD_DIM), lambda b, h: (b, 0, off // HEAD_DIM + h // div))
    vec = pl.BlockSpec((1, HEAD_DIM), lambda b, h: (0, 0))
    return pl.pallas_call(
        functools.partial(_ctx_attn_kernel, norm=norm),
        grid=(B, n_heads),
        in_specs=[spec(off_q, 1), spec(off_k, group), spec(off_v, group), vec, vec],
        out_specs=pl.BlockSpec((1, L, HEAD_DIM), lambda b, h: (b, 0, h)),
        out_shape=jax.ShapeDtypeStruct((B, L, n_heads * HEAD_DIM), BF16),
        compiler_params=_params("parallel", "parallel"),
        name="ctx_attention",
    )(zc3, zc3, zc3, qn, kn)


SSM_SLABS = SSM_WIDTH // LANES
SLAB_STATE = SSM_LANES // SSM_SLABS


def _s5_kernel(uf_ref, ub_ref, wb_ref, wc_ref, a_ref, h0_ref, yf_ref, yb_ref, hout_ref,
               br, bi, hs, *, tc, n, nb):
    j = pl.program_id(0)
    rows = tc * nb

    @pl.when(j == 0)
    def _():
        hs[...] = h0_ref[...]

    for d, (u_ref, y_ref) in enumerate(((uf_ref, yf_ref), (ub_ref, yb_ref))):
        u = u_ref[...].reshape(rows, SSM_WIDTH).astype(BF16)
        for k in range(SSM_SLABS):
            r = jnp.dot(u[:, k * LANES:(k + 1) * LANES], wb_ref[d, k], preferred_element_type=F32)
            br[:, k * SLAB_STATE:(k + 1) * SLAB_STATE] = r[:, :SLAB_STATE]
            bi[:, k * SLAB_STATE:(k + 1) * SLAB_STATE] = r[:, SLAB_STATE:]
        ar = jnp.broadcast_to(a_ref[d, 0], (nb, SSM_LANES))
        ai = jnp.broadcast_to(a_ref[d, 1], (nb, SSM_LANES))

        def step(i, carry, d=d, ar=ar, ai=ai):
            hr, hi = carry
            t = i if d == 0 else tc - 1 - i
            row = pl.multiple_of(t * nb, nb)
            nr = ar * hr - ai * hi + br[pl.ds(row, nb), :]
            ni = ar * hi + ai * hr + bi[pl.ds(row, nb), :]
            br[pl.ds(row, nb), :] = nr
            bi[pl.ds(row, nb), :] = ni
            return nr, ni

        hr, hi = lax.fori_loop(0, tc, step, (hs[2 * d], hs[2 * d + 1]))
        hs[2 * d] = hr
        hs[2 * d + 1] = hi
        for k in range(SSM_SLABS):
            sl = slice(k * SLAB_STATE, (k + 1) * SLAB_STATE)
            y = (jnp.dot(br[:, sl].astype(BF16), wc_ref[d, k, 0:SLAB_STATE, :], preferred_element_type=F32)
                 + jnp.dot(bi[:, sl].astype(BF16), wc_ref[d, k, SLAB_STATE:, :], preferred_element_type=F32))
            y_ref[:, :, k * LANES:(k + 1) * LANES] = y.reshape(tc, nb, LANES)

    @pl.when(j == n - 1)
    def _():
        hout_ref[...] = hs[...]


def _s5_scan(u_t, wb, wc, a, h0):
    T, nb, _ = u_t.shape
    tc = _tile(T, 64)
    n = T // tc
    ublk = lambda rev: pl.BlockSpec((tc, nb, SSM_WIDTH), (lambda j: (n - 1 - j, 0, 0)) if rev else (lambda j: (j, 0, 0)))
    const = lambda shape: pl.BlockSpec(shape, lambda j: (0,) * len(shape))
    return pl.pallas_call(
        functools.partial(_s5_kernel, tc=tc, n=n, nb=nb),
        grid=(n,),
        in_specs=[ublk(False), ublk(True), const(wb.shape), const(wc.shape), const(a.shape), const(h0.shape)],
        out_specs=[ublk(False), ublk(True), const(h0.shape)],
        out_shape=[jax.ShapeDtypeStruct((T, nb, SSM_WIDTH), F32), jax.ShapeDtypeStruct((T, nb, SSM_WIDTH), F32),
                   jax.ShapeDtypeStruct(h0.shape, F32)],
        scratch_shapes=[pltpu.VMEM((tc * nb, SSM_LANES), F32), pltpu.VMEM((tc * nb, SSM_LANES), F32),
                        pltpu.VMEM(h0.shape, F32)],
        compiler_params=_params("arbitrary"),
        name="s5_scan",
    )(u_t, u_t, wb, wc, a, h0)


def _s5_weights(lam_r, lam_i, log_dt, b_r, b_i, c_r, c_i):
    lam_r, lam_i, log_dt = lam_r.astype(F32), lam_i.astype(F32), log_dt.astype(F32)
    dt = jnp.exp(log_dt)[..., None]
    mag = jnp.exp(lam_r * dt)
    ar, ai = mag * jnp.cos(lam_i * dt), mag * jnp.sin(lam_i * dt)
    den = lam_r * lam_r + lam_i * lam_i
    nr, ni = ar - 1.0, ai
    kr, ki = (nr * lam_r + ni * lam_i) / den, (ni * lam_r - nr * lam_i) / den
    b_r, b_i = b_r.astype(F32), b_i.astype(F32)
    bbr = kr[..., None] * b_r - ki[..., None] * b_i
    bbi = kr[..., None] * b_i + ki[..., None] * b_r
    gs = SSM_GROUPS // SSM_SLABS
    eye = jnp.eye(gs, dtype=F32)

    def in_slab(w):
        w = w.reshape(2, SSM_SLABS, gs, SSM_STATE, SSM_GROUP)
        return jnp.einsum('dkgph,gj->dkghjp', w, eye).reshape(2, SSM_SLABS, gs * SSM_GROUP, gs * SSM_STATE)

    def out_slab(w):
        w = w.reshape(2, SSM_SLABS, gs, SSM_GROUP, SSM_STATE)
        return jnp.einsum('dkghp,gj->dkjpgh', w, eye).reshape(2, SSM_SLABS, gs * SSM_STATE, gs * SSM_GROUP)

    wb = jnp.concatenate([in_slab(bbr), in_slab(bbi)], axis=-1).astype(BF16)
    wc = jnp.concatenate([out_slab(c_r.astype(F32)), -out_slab(c_i.astype(F32))], axis=-2).astype(BF16)
    a = jnp.stack([ar.reshape(2, 1, SSM_LANES), ai.reshape(2, 1, SSM_LANES)], axis=1)
    return wb, wc, a


def _gelu_tanh(x):
    return 0.5 * x * (1.0 + jnp.tanh(math.sqrt(2.0 / math.pi) * (x + 0.044715 * (x * x * x))))


def _glu_kernel(u_ref, yf_ref, yb_ref, d_ref, w_ref, b_ref, o_ref):
    y = u_ref[0].astype(F32) * d_ref[...] + yf_ref[...] + yb_ref[...]
    t = _gelu_tanh(y).astype(BF16)
    r = jnp.dot(t, w_ref[...], preferred_element_type=F32) + b_ref[...]
    o_ref[0] = (r[:, :SSM_WIDTH] * _sigmoid(r[:, SSM_WIDTH:])).astype(o_ref.dtype)


def _glu(z3, yf, yb, d, w, b):
    B, S, _ = z3.shape
    ts = _tile(S, 512)
    y2 = lambda y: y.reshape(S, B * SSM_WIDTH)
    return pl.pallas_call(
        _glu_kernel,
        grid=(B, S // ts),
        in_specs=[pl.BlockSpec((1, ts, SSM_WIDTH), lambda b, i: (b, i, OFF_SU // SSM_WIDTH)),
                  pl.BlockSpec((ts, SSM_WIDTH), lambda b, i: (i, b)),
                  pl.BlockSpec((ts, SSM_WIDTH), lambda b, i: (i, b)),
                  pl.BlockSpec((1, SSM_WIDTH), lambda b, i: (0, 0)),
                  pl.BlockSpec((SSM_WIDTH, 2 * SSM_WIDTH), lambda b, i: (0, 0)),
                  pl.BlockSpec((1, 2 * SSM_WIDTH), lambda b, i: (0, 0))],
        out_specs=pl.BlockSpec((1, ts, SSM_WIDTH), lambda b, i: (b, i, 0)),
        out_shape=jax.ShapeDtypeStruct((B, S, SSM_WIDTH), BF16),
        compiler_params=_params("parallel", "parallel"),
        name="s5_glu",
    )(z3, y2(yf), y2(yb), d, w, b)


def _merge_kernel(a_ref, b_ref, s_ref, ga_ref, gb_ref, gs_ref, wa_ref, wb_ref, ws_ref, o_ref):
    m = (_sigmoid(ga_ref[...].astype(F32)) * jnp.dot(a_ref[...], wa_ref[...], preferred_element_type=F32)
         + _sigmoid(gb_ref[...].astype(F32)) * jnp.dot(b_ref[...], wb_ref[...], preferred_element_type=F32)
         + _sigmoid(gs_ref[...].astype(F32)) * jnp.dot(s_ref[...], ws_ref[...], preferred_element_type=F32))
    o_ref[...] = m.astype(o_ref.dtype)


def _merge(a, b, s, z2, wa, wb, ws):
    M = a.shape[0]
    D = wa.shape[1]
    tm, tn = _tile(M, 1024), _tile(D, 1024)
    row = lambda w: pl.BlockSpec((tm, w), lambda i, j: (i, 0))
    gate = lambda br: pl.BlockSpec((tm, tn), lambda i, j: (i, (OFF_GATE + br * D) // tn + j))
    wsp = lambda w: pl.BlockSpec((w, tn), lambda i, j: (0, j))
    return pl.pallas_call(
        _merge_kernel,
        grid=(M // tm, D // tn),
        in_specs=[row(NA_W), row(GQ_W), row(SSM_WIDTH), gate(0), gate(1), gate(2),
                  wsp(NA_W), wsp(GQ_W), wsp(SSM_WIDTH)],
        out_specs=pl.BlockSpec((tm, tn), lambda i, j: (i, j)),
        out_shape=jax.ShapeDtypeStruct((M, D), BF16),
        compiler_params=_params("parallel", "arbitrary"),
        name="merge",
    )(a, b, s, z2, z2, z2, wa, wb, ws)


def _layer_norm(x, g, b):
    mu = jnp.mean(x, axis=-1, keepdims=True)
    xc = x - mu
    var = jnp.mean(xc * xc, axis=-1, keepdims=True)
    return xc * lax.rsqrt(var + LN_EPS) * g + b


def _outproj_kernel(m_ref, w_ref, h_ref, g1_ref, sh_ref, sc_ref, lng_ref, lnb_ref, rwh_ref, rwl_ref, rb_ref,
                    cnt0_ref, hn_ref, tokr_ref, tokb_ref, idx_ref, wts_ref, rank_ref, cnt_ref, cnt, *, tm, D):
    @pl.when(pl.program_id(0) == 0)
    def _():
        cnt[...] = cnt0_ref[...]

    mix = jnp.dot(m_ref[...], w_ref[...], preferred_element_type=F32)
    hn = _layer_norm(DN_ALPHA * h_ref[...] + g1_ref[0] * mix, lng_ref[...], lnb_ref[...])
    hn_ref[...] = hn
    tok = hn * (1.0 + sc_ref[0]) + sh_ref[0]
    tokr_ref[...] = tok
    t_hi = tok.astype(BF16)
    tokb_ref[...] = t_hi
    t_lo = (tok - t_hi.astype(F32)).astype(BF16)
    logits = (jnp.dot(t_hi, rwh_ref[...], preferred_element_type=F32)
              + jnp.dot(t_lo, rwh_ref[...], preferred_element_type=F32)
              + jnp.dot(t_hi, rwl_ref[...], preferred_element_type=F32))
    scores = _sigmoid(logits)
    sel = scores + rb_ref[...]
    lane = lax.broadcasted_iota(jnp.int32, sel.shape, 1).astype(F32)
    slot = lax.broadcasted_iota(jnp.int32, (tm, LANES), 1)
    idx_acc = jnp.zeros((tm, LANES), F32)
    w_acc = jnp.zeros((tm, LANES), F32)
    hits = []
    for k in range(TOP_K):
        mx = jnp.max(sel, axis=-1, keepdims=True)
        am = jnp.min(jnp.where(sel == mx, lane, float(N_EXPERTS)), axis=-1, keepdims=True)
        hit = lane == am
        hits.append(hit)
        wk = jnp.sum(jnp.where(hit, scores, 0.0), axis=-1, keepdims=True)
        idx_acc = jnp.where(slot == k, am, idx_acc)
        w_acc = jnp.where(slot == k, wk, w_acc)
        sel = jnp.where(hit, -jnp.inf, sel)
    wsum = jnp.sum(w_acc, axis=-1, keepdims=True)
    idx_ref[...] = idx_acc.astype(jnp.int32)
    wts_ref[...] = w_acc / wsum * ROUTED_SCALE

    picked = jnp.zeros(sel.shape, F32)
    for hit in hits:
        picked = jnp.where(hit, 1.0, picked)
    ri = lax.broadcasted_iota(jnp.int32, (tm, tm), 0)
    ci = lax.broadcasted_iota(jnp.int32, (tm, tm), 1)
    below = jnp.where(ci < ri, 1.0, 0.0).astype(BF16)
    before = jnp.dot(below, picked.astype(BF16), preferred_element_type=F32) + cnt[...]
    rank_acc = jnp.zeros((tm, LANES), F32)
    for k, hit in enumerate(hits):
        rk = jnp.sum(jnp.where(hit, before, 0.0), axis=-1, keepdims=True)
        rank_acc = jnp.where(slot == k, rk, rank_acc)
    rank_ref[...] = rank_acc.astype(jnp.int32)
    cnt[...] = cnt[...] + jnp.sum(picked, axis=0, keepdims=True)
    cnt_ref[...] = cnt[...]


def _outproj_ln_route(m, w_out, h2, mod3, row_fn, ln_g, ln_b, rw_hi, rw_lo, rbias, cnt0, tm=256):
    M, D = m.shape
    tm = _tile(M, tm)
    vec = lambda k: pl.BlockSpec((1, 1, D), lambda i: (row_fn(i), 0, k))
    const = lambda shape: pl.BlockSpec(shape, lambda i: (0,) * len(shape))
    rowblk = lambda w: pl.BlockSpec((tm, w), lambda i: (i, 0))
    return pl.pallas_call(
        functools.partial(_outproj_kernel, tm=tm, D=D),
        grid=(M // tm,),
        in_specs=[rowblk(D), const((D, D)), rowblk(D), vec(2), vec(3), vec(4), const((1, D)), const((1, D)),
                  const((D, N_EXPERTS)), const((D, N_EXPERTS)), const((1, N_EXPERTS)), const((1, N_EXPERTS))],
        out_specs=[rowblk(D), rowblk(D), rowblk(D),
                   rowblk(LANES), rowblk(LANES), rowblk(LANES), const((1, N_EXPERTS))],
        out_shape=[jax.ShapeDtypeStruct((M, D), F32), jax.ShapeDtypeStruct((M, D), F32),
                   jax.ShapeDtypeStruct((M, D), BF16), jax.ShapeDtypeStruct((M, LANES), jnp.int32),
                   jax.ShapeDtypeStruct((M, LANES), F32), jax.ShapeDtypeStruct((M, LANES), jnp.int32),
                   jax.ShapeDtypeStruct((1, N_EXPERTS), F32)],
        scratch_shapes=[pltpu.VMEM((1, N_EXPERTS), F32)],
        compiler_params=_params("arbitrary"),
        name="outproj_ln_route",
    )(m, w_out, h2, mod3, mod3, mod3, ln_g, ln_b, rw_hi, rw_lo, rbias, cnt0)


def _swiglu_kernel(x_ref, w13_ref, w2_ref, o_ref, *, E):
    h = jnp.dot(x_ref[...], w13_ref[...], preferred_element_type=F32)
    g = h[:, :E]
    act = (g * _sigmoid(g) * h[:, E:]).astype(BF16)
    o_ref[...] = jnp.dot(act, w2_ref[...], preferred_element_type=F32).astype(o_ref.dtype)


def _shared_expert(tokb, w13, w2):
    T, D = tokb.shape
    E = w2.shape[0]
    tm = _tile(T, 512)
    return pl.pallas_call(
        functools.partial(_swiglu_kernel, E=E),
        grid=(T // tm,),
        in_specs=[pl.BlockSpec((tm, D), lambda i: (i, 0)), pl.BlockSpec((D, 2 * E), lambda i: (0, 0)),
                  pl.BlockSpec((E, D), lambda i: (0, 0))],
        out_specs=pl.BlockSpec((tm, D), lambda i: (i, 0)),
        out_shape=jax.ShapeDtypeStruct((T, D), BF16),
        compiler_params=_params("parallel"),
        name="shared_expert",
    )(tokb, w13, w2)


def _route_plan(idx, rank, counts, tm):
    T = idx.shape[0]
    counts = counts.reshape(N_EXPERTS).astype(jnp.int32)
    padded = (counts + tm - 1) // tm * tm
    pend = jnp.cumsum(padded)
    pstart = pend - padded
    dest = (pstart[idx] + rank).astype(jnp.int32)
    n_blk = -(-(T * TOP_K + N_EXPERTS * (tm - 1)) // tm)
    first = jnp.arange(n_blk, dtype=jnp.int32) * tm
    blk_exp = jnp.minimum(jnp.sum(first[:, None] >= pend[None, :], axis=1), N_EXPERTS - 1).astype(jnp.int32)
    n_used = (pend[-1:] // tm).astype(jnp.int32)
    return dest, blk_exp, n_used, (pstart + counts).astype(jnp.int32), (padded - counts).astype(jnp.int32)


def _dispatch_kernel(ps_ref, pc_ref, nu_ref, dest_ref, tok_ref, xs_hbm, zbuf, sem, *, tb, n, tm, n_blk):
    i = pl.program_id(0)

    def pad_rows(do):
        def per_expert(e, c):
            def per_row(r, c2):
                do(pltpu.make_async_copy(zbuf.at[pl.ds(0, 1)], xs_hbm.at[pl.ds(ps_ref[e] + r, 1)], sem.at[1]))
                return c2
            return lax.fori_loop(0, pc_ref[e], per_row, c)
        lax.fori_loop(0, N_EXPERTS, per_expert, 0)

        def per_block(b, c):
            do(pltpu.make_async_copy(zbuf, xs_hbm.at[pl.ds(pl.multiple_of(b * tm, tm), tm)], sem.at[1]))
            return c
        lax.fori_loop(nu_ref[0], n_blk, per_block, 0)

    @pl.when(i == 0)
    def _():
        zbuf[...] = jnp.zeros(zbuf.shape, zbuf.dtype)
        pad_rows(lambda cp: cp.start())

    def body(t, c):
        src = tok_ref.at[pl.ds(t, 1)]
        for k in range(TOP_K):
            d = dest_ref[0, 0, t * TOP_K + k]
            pltpu.make_async_copy(src, xs_hbm.at[pl.ds(d, 1)], sem.at[0]).start()
        return c
    lax.fori_loop(0, tb, body, 0)
    for k in range(TOP_K):
        pltpu.make_async_copy(tok_ref, xs_hbm.at[pl.ds(0, tb)], sem.at[0]).wait()

    @pl.when(i == n - 1)
    def _():
        pad_rows(lambda cp: cp.wait())


def _dispatch(tokr, dest, pad_start, pad_cnt, n_used, n_blk, tm, tb=256):
    T, D = tokr.shape
    tb = _tile(T, tb)
    n = T // tb
    grid_spec = pltpu.PrefetchScalarGridSpec(
        num_scalar_prefetch=3,
        grid=(n,),
        in_specs=[pl.BlockSpec((1, 1, tb * TOP_K), lambda i, ps, pc, nu: (i, 0, 0), memory_space=pltpu.SMEM),
                  pl.BlockSpec((tb, D), lambda i, ps, pc, nu: (i, 0))],
        out_specs=pl.BlockSpec(memory_space=pl.ANY),
        scratch_shapes=[pltpu.VMEM((tm, D), F32), pltpu.SemaphoreType.DMA((2,))],
    )
    return pl.pallas_call(
        functools.partial(_dispatch_kernel, tb=tb, n=n, tm=tm, n_blk=n_blk),
        grid_spec=grid_spec,
        out_shape=jax.ShapeDtypeStruct((n_blk * tm, D), F32),
        compiler_params=pltpu.CompilerParams(dimension_semantics=("arbitrary",), vmem_limit_bytes=VMEM_LIMIT,
                                             disable_bounds_checks=True),
        name="moe_dispatch",
    )(pad_start, pad_cnt, n_used, dest.reshape(n, 1, tb * TOP_K), tokr)


def _expert_kernel(be_ref, nu_ref, x_ref, w1_ref, w3_ref, w2_ref, y_ref, w13s, w2s, *, E):
    i = pl.program_id(0)

    @pl.when((i == 0) | (be_ref[i] != be_ref[jnp.maximum(i - 1, 0)]))
    def _():
        w13s[:, 0:E] = w1_ref[...].astype(BF16)
        w13s[:, E:2 * E] = w3_ref[...].astype(BF16)
        w2s[...] = w2_ref[...].astype(BF16)

    @pl.when(i < nu_ref[0])
    def _():
        h = jnp.dot(x_ref[...].astype(BF16), w13s[...], preferred_element_type=F32)
        g = h[:, :E]
        act = (g * _sigmoid(g) * h[:, E:]).astype(BF16)
        y_ref[...] = jnp.dot(act, w2s[...], preferred_element_type=F32)

    @pl.when(i >= nu_ref[0])
    def _():
        y_ref[...] = jnp.zeros(y_ref.shape, y_ref.dtype)


def _routed_experts(xs, blk_exp, n_used, w1, w3, w2, l, tm):
    n = blk_exp.shape[0]
    _, _, D, E = w1.shape
    wspec = lambda a, b: pl.BlockSpec((None, None, a, b), lambda i, be, nu: (l, be[i], 0, 0))
    grid_spec = pltpu.PrefetchScalarGridSpec(
        num_scalar_prefetch=2,
        grid=(n,),
        in_specs=[pl.BlockSpec((tm, D), lambda i, be, nu: (jnp.where(i < nu[0], i, 0), 0)),
                  wspec(D, E), wspec(D, E), wspec(E, D)],
        out_specs=pl.BlockSpec((tm, D), lambda i, be, nu: (i, 0)),
        scratch_shapes=[pltpu.VMEM((D, 2 * E), BF16), pltpu.VMEM((E, D), BF16)],
    )
    return pl.pallas_call(
        functools.partial(_expert_kernel, E=E),
        grid_spec=grid_spec,
        out_shape=jax.ShapeDtypeStruct((n * tm, D), F32),
        compiler_params=_params("arbitrary"),
        name="routed_experts",
    )(blk_exp, n_used, xs, w1, w3, w2)


def _combine_kernel(dcur_ref, dnxt_ref, h_ref, sh_ref, w_ref, g2_ref, lng_ref, lnb_ref, ys_hbm, o_ref,
                    ybuf, sem, *, tb, D, n):
    i = pl.program_id(0)
    rows = tb * TOP_K

    def gather(slot, d_ref):
        def body(g, c):
            t0 = pl.multiple_of(g * SUBLANES, SUBLANES)
            dst0 = pl.multiple_of(slot * rows + t0, SUBLANES)
            for u in range(SUBLANES):
                for k in range(TOP_K):
                    d = d_ref[0, 0, (t0 + u) * TOP_K + k]
                    pltpu.make_async_copy(ys_hbm.at[pl.ds(d, 1)], ybuf.at[pl.ds(dst0 + (k * tb + u), 1)],
                                          sem.at[slot]).start()
            return c
        lax.fori_loop(0, tb // SUBLANES, body, 0)

    @pl.when(i == 0)
    def _():
        gather(0, dcur_ref)

    @pl.when(i + 1 < n)
    def _():
        gather((i + 1) % 2, dnxt_ref)

    slot = i % 2
    base = slot * rows
    pltpu.make_async_copy(ys_hbm.at[pl.ds(0, rows)], ybuf.at[pl.ds(pl.multiple_of(base, rows), rows)],
                          sem.at[slot]).wait()
    w = w_ref[...]
    f = sh_ref[...].astype(F32)
    for k in range(TOP_K):
        f = f + w[:, k:k + 1] * ybuf[pl.ds(pl.multiple_of(base + k * tb, tb), tb), :]
    o_ref[...] = _layer_norm(DN_ALPHA * h_ref[...] + g2_ref[0] * f, lng_ref[...], lnb_ref[...])


def _combine_ln(h2, shared, wts, ys, dest, tok_off, mod3, row_fn, ln_g, ln_b, tb=128):
    M, D = h2.shape
    tb = _tile(M, tb)
    n = M // tb
    off = tok_off // tb
    dest3 = dest.reshape(-1, 1, tb * TOP_K)
    rowblk = lambda w, o: pl.BlockSpec((tb, w), lambda i: (i + o, 0))
    const = lambda shape: pl.BlockSpec(shape, lambda i: (0,) * len(shape))
    smem = lambda fn: pl.BlockSpec((1, 1, tb * TOP_K), fn, memory_space=pltpu.SMEM)
    return pl.pallas_call(
        functools.partial(_combine_kernel, tb=tb, D=D, n=n),
        grid=(n,),
        in_specs=[smem(lambda i: (i + off, 0, 0)), smem(lambda i: (jnp.minimum(i + 1, n - 1) + off, 0, 0)),
                  rowblk(D, 0), rowblk(D, off), rowblk(LANES, off),
                  pl.BlockSpec((1, 1, D), lambda i: (row_fn(i), 0, 5)), const((1, D)), const((1, D)),
                  pl.BlockSpec(memory_space=pl.ANY)],
        out_specs=rowblk(D, 0),
        out_shape=jax.ShapeDtypeStruct((M, D), F32),
        scratch_shapes=[pltpu.VMEM((2 * tb * TOP_K, D), F32), pltpu.SemaphoreType.DMA((2,))],
        compiler_params=pltpu.CompilerParams(dimension_semantics=("arbitrary",), vmem_limit_bytes=VMEM_LIMIT,
                                             disable_bounds_checks=True),
        name="combine_ln",
    )(dest3, dest3, h2, shared, wts, mod3, ln_g, ln_b, ys)


MOE_TM = 256


def kernel(x, c, ctx, c_ctx, w_mod, b_mod, w_in, na_rpb, gq_q_norm, gq_k_norm, ssm_lam_re, ssm_lam_im, ssm_log_dt, ssm_b_re, ssm_b_im, ssm_c_re, ssm_c_im, ssm_d, ssm_glu_w, ssm_glu_b, w_br_na, w_br_gq, w_br_ssm, w_out, ln1_g, ln1_b, router_w, router_bias, exp_w1, exp_w3, exp_w2, sh_w1, sh_w3, sh_w2, ln2_g, ln2_b):
    B, S, D = x.shape
    L = ctx.shape[1]
    assert B < 2 * SUBLANES and D % LANES == 0
    cos, sin = _rope_tables(S)
    c_all = jnp.zeros((2 * SUBLANES, D), F32).at[:B].set(c).at[B].set(c_ctx)
    lat_row = lambda b: b
    ctx_row = lambda b: B
    h, hc = x, ctx
    for l in range(DEPTH):
        need_ctx = l < DEPTH - 1
        mod3 = _modvec(c_all, w_mod, l, b_mod[l])[:, None, :]
        vec = lambda p: p[l].reshape(1, -1).astype(F32)

        w_in_b = w_in[l].astype(BF16)
        u = _modulate(h, mod3, lat_row, 0, 1)
        uc = _modulate(hc, mod3, ctx_row, 0, 1)
        z2 = _matmul(u.reshape(B * S, D), w_in_b, BF16)
        zc2 = _matmul(uc.reshape(B * L, D), w_in_b if need_ctx else w_in_b[:, :CTX_IN_W], BF16)
        z3, zc3 = z2.reshape(B, S, -1), zc2.reshape(B, L, -1)

        a_out = _na_attention(z3, zc3, _na_bias_table(na_rpb[l], S // GRID_W))
        qn, kn = vec(gq_q_norm), vec(gq_k_norm)
        b_out = _gqa_attention(z3, zc3, cos, sin, qn, kn)

        wb, wc, a = _s5_weights(ssm_lam_re[l], ssm_lam_im[l], ssm_log_dt[l], ssm_b_re[l], ssm_b_im[l],
                                ssm_c_re[l], ssm_c_im[l])
        su = lambda t: t[:, :, OFF_SU:OFF_SU + SSM_WIDTH].astype(F32).transpose(1, 0, 2)
        h0 = jnp.zeros((4, B, SSM_LANES), F32)
        ycf, ycb, hfin = _s5_scan(su(zc3), wb, wc, a, h0)
        yf, yb, _ = _s5_scan(su(z3), wb, wc, a, hfin)
        glu_w, glu_b, dvec = ssm_glu_w[l].astype(BF16), vec(ssm_glu_b), vec(ssm_d)
        c_out = _glu(z3, yf, yb, dvec, glu_w, glu_b)

        wa, wg, ws, wo = (w[l].astype(BF16) for w in (w_br_na, w_br_gq, w_br_ssm, w_out))
        m = _merge(a_out.reshape(B * S, -1), b_out.reshape(B * S, -1), c_out.reshape(B * S, -1), z2, wa, wg, ws)
        rw = router_w[l].astype(F32)
        rw_hi = rw.astype(BF16)
        rw_lo = (rw - rw_hi.astype(F32)).astype(BF16)
        rbias = vec(router_bias)
        g1, b1 = vec(ln1_g), vec(ln1_b)
        tm_o = _tile(S, 256)
        h2, tokr, tokb, idx, wts, rank, counts = _outproj_ln_route(
            m, wo, h.reshape(B * S, D), mod3, lambda i: i // (S // tm_o), g1, b1, rw_hi, rw_lo, rbias,
            jnp.zeros((1, N_EXPERTS), F32), tm=tm_o)
        if need_ctx:
            a_ctx = _ctx_attention(zc3, OFF_NA_Q, OFF_NA_K, OFF_NA_V, 1, False, qn, kn)
            b_ctx = _ctx_attention(zc3, OFF_GQ_Q, OFF_GQ_K, OFF_GQ_V, GQ_GROUP, True, qn, kn)
            c_ctx_out = _glu(zc3, ycf, ycb, dvec, glu_w, glu_b)
            mc = _merge(a_ctx.reshape(B * L, -1), b_ctx.reshape(B * L, -1), c_ctx_out.reshape(B * L, -1),
                        zc2, wa, wg, ws)
            tm_c = _tile(L, 256)
            hc2, tokr_c, tokb_c, idx_c, wts_c, rank_c, counts = _outproj_ln_route(
                mc, wo, hc.reshape(B * L, D), mod3, lambda i: B, g1, b1, rw_hi, rw_lo, rbias, counts, tm=tm_c)
            tokr = jnp.concatenate([tokr, tokr_c], axis=0)
            tokb = jnp.concatenate([tokb, tokb_c], axis=0)
            idx = jnp.concatenate([idx, idx_c], axis=0)
            wts = jnp.concatenate([wts, wts_c], axis=0)
            rank = jnp.concatenate([rank, rank_c], axis=0)

        w13 = jnp.concatenate([sh_w1[l], sh_w3[l]], axis=1).astype(BF16)
        shared = _shared_expert(tokb, w13, sh_w2[l].astype(BF16))
        dest, blk_exp, n_used, pad_start, pad_cnt = _route_plan(idx[:, :TOP_K], rank[:, :TOP_K], counts, MOE_TM)
        xs = _dispatch(tokr, dest, pad_start, pad_cnt, n_used, blk_exp.shape[0], MOE_TM)
        ys = _routed_experts(xs, blk_exp, n_used, exp_w1, exp_w3, exp_w2, l, MOE_TM)
        g2, b2 = vec(ln2_g), vec(ln2_b)
        tb_l = _tile(S, 128)
        h = _combine_ln(h2, shared, wts, ys, dest, 0, mod3, lambda i: i // (S // tb_l), g2, b2,
                        tb=tb_l).reshape(B, S, D)
        if need_ctx:
            hc = _combine_ln(hc2, shared, wts, ys, dest, B * S, mod3, lambda i: B, g2, b2,
                             tb=_tile(L, 128)).reshape(B, L, D)
    return h
```

```python
import functools
import math

import jax
import jax.numpy as jnp
import numpy as np
from jax import lax
from jax.experimental import pallas as pl
from jax.experimental.pallas import tpu as pltpu

F32 = jnp.float32
BF16 = jnp.bfloat16

DEPTH = 2
GRID_W = 64
HEAD_DIM = 128
NA_HEADS = 6
NA_WIN_ROWS = 8
NA_WIN_COLS = 16
GQ_HEADS = 6
GQ_KV_HEADS = 2
GQ_GROUP = GQ_HEADS // GQ_KV_HEADS
ROPE_THETA = 10000.0
SSM_GROUP = 16
SSM_WIDTH = 512
SSM_GROUPS = SSM_WIDTH // SSM_GROUP
SSM_STATE = 64
SSM_LANES = SSM_GROUPS * SSM_STATE
N_EXPERTS = 64
TOP_K = 8
EXPERT_DIM = 512
ROUTED_SCALE = 2.5
DN_ALPHA = (2 * DEPTH) ** 0.25
LN_EPS = 1e-6
RMS_EPS = 1e-6
ATT_SCALE = HEAD_DIM ** -0.5
NEG = -1e30
NA_QROWS = 8
NA_KROWS = 16

LANES = 128
SUBLANES = 8
VMEM_LIMIT = 56 * 1024 * 1024

NA_W = NA_HEADS * HEAD_DIM
GQ_W = GQ_HEADS * HEAD_DIM
GQ_KV_W = GQ_KV_HEADS * HEAD_DIM
OFF_NA_K = 0
OFF_NA_V = OFF_NA_K + NA_W
OFF_GQ_K = OFF_NA_V + NA_W
OFF_GQ_V = OFF_GQ_K + GQ_KV_W
OFF_SU = OFF_GQ_V + GQ_KV_W
CTX_IN_W = OFF_SU + SSM_WIDTH
OFF_NA_Q = CTX_IN_W
OFF_GQ_Q = OFF_NA_Q + NA_W
OFF_GATE = OFF_GQ_Q + GQ_W


def _tile(n, pref):
    t = min(n, pref)
    while n % t:
        t //= 2
    return t


def _params(*sem):
    return pltpu.CompilerParams(dimension_semantics=sem, vmem_limit_bytes=VMEM_LIMIT)


def _sigmoid(x):
    return 1.0 / (1.0 + jnp.exp(-x))


def _modvec_kernel(c_ref, w_ref, b_ref, o_ref):
    c = c_ref[...]
    a = (c * _sigmoid(c)).astype(BF16)
    o_ref[...] = jnp.dot(a, w_ref[...].astype(BF16), preferred_element_type=F32) + b_ref[...]


def _modvec(c_all, w, l, b):
    R, D = c_all.shape
    N = w.shape[2]
    tn = _tile(N, 1024)
    return pl.pallas_call(
        _modvec_kernel,
        grid=(N // tn,),
        in_specs=[pl.BlockSpec((R, D), lambda j: (0, 0)),
                  pl.BlockSpec((None, D, tn), lambda j: (l, 0, j)),
                  pl.BlockSpec((1, tn), lambda j: (0, j))],
        out_specs=pl.BlockSpec((R, tn), lambda j: (0, j)),
        out_shape=jax.ShapeDtypeStruct((R, N), F32),
        compiler_params=_params("arbitrary"),
        name="modvec",
    )(c_all, w, b.reshape(1, N))


def _modulate_kernel(h_ref, sh_ref, sc_ref, o_ref):
    o_ref[0] = (h_ref[0] * (1.0 + sc_ref[0]) + sh_ref[0]).astype(o_ref.dtype)


def _modulate(h, mod3, row_fn, k_shift, k_scale):
    B, S, D = h.shape
    ts = _tile(S, 512)
    return pl.pallas_call(
        _modulate_kernel,
        grid=(B, S // ts),
        in_specs=[pl.BlockSpec((1, ts, D), lambda b, i: (b, i, 0)),
                  pl.BlockSpec((1, 1, D), lambda b, i: (row_fn(b), 0, k_shift)),
                  pl.BlockSpec((1, 1, D), lambda b, i: (row_fn(b), 0, k_scale))],
        out_specs=pl.BlockSpec((1, ts, D), lambda b, i: (b, i, 0)),
        out_shape=jax.ShapeDtypeStruct((B, S, D), BF16),
        compiler_params=_params("parallel", "parallel"),
        name="modulate",
    )(h, mod3, mod3)


def _mm_kernel(x_ref, w_ref, o_ref):
    o_ref[...] = jnp.dot(x_ref[...], w_ref[...], preferred_element_type=F32).astype(o_ref.dtype)


def _matmul(x, w, out_dtype, tm=1024, tn=1024):
    M, K = x.shape
    N = w.shape[1]
    tm, tn = _tile(M, tm), _tile(N, tn)
    return pl.pallas_call(
        _mm_kernel,
        grid=(M // tm, N // tn),
        in_specs=[pl.BlockSpec((tm, K), lambda i, j: (i, 0)),
                  pl.BlockSpec((K, tn), lambda i, j: (0, j))],
        out_specs=pl.BlockSpec((tm, tn), lambda i, j: (i, j)),
        out_shape=jax.ShapeDtypeStruct((M, N), out_dtype),
        compiler_params=_params("parallel", "arbitrary"),
        name="matmul",
    )(x, w)


def _nt_dot(a, b):
    return lax.dot_general(a, b, (((1,), (1,)), ((), ())), preferred_element_type=F32)


def _na_key_row0(blk, rows):
    return min(max(NA_QROWS * blk - NA_WIN_ROWS // 2, 0), rows - NA_KROWS)


def _na_kernel(q_ref, k_ref, v_ref, kc_ref, vc_ref, tab_ref, o_ref, *, rows):
    kc = kc_ref[0]
    vc = vc_ref[0]
    nq = NA_QROWS * GRID_W
    for blk in range(rows // NA_QROWS):
        k0 = _na_key_row0(blk, rows) * GRID_W
        q = q_ref[0, blk * nq:(blk + 1) * nq, :]
        kb = k_ref[0, k0:k0 + NA_KROWS * GRID_W, :]
        vb = v_ref[0, k0:k0 + NA_KROWS * GRID_W, :]
        s1 = _nt_dot(q, kb) * ATT_SCALE + tab_ref[0, blk]
        s2 = _nt_dot(q, kc) * ATT_SCALE
        m = jnp.maximum(jnp.max(s1, axis=-1, keepdims=True), jnp.max(s2, axis=-1, keepdims=True))
        p1 = jnp.exp(s1 - m)
        p2 = jnp.exp(s2 - m)
        l = jnp.sum(p1, axis=-1, keepdims=True) + jnp.sum(p2, axis=-1, keepdims=True)
        o = (jnp.dot(p1.astype(BF16), vb, preferred_element_type=F32)
             + jnp.dot(p2.astype(BF16), vc, preferred_element_type=F32))
        o_ref[0, blk * nq:(blk + 1) * nq, :] = (o / l).astype(o_ref.dtype)


def _na_attention(z3, zc3, tab):
    B, S, _ = z3.shape
    L = zc3.shape[1]
    rows = S // GRID_W
    assert rows >= NA_KROWS and rows % NA_QROWS == 0
    blk = lambda n, off: pl.BlockSpec((1, n, HEAD_DIM), lambda h, b: (b, 0, off // HEAD_DIM + h))
    return pl.pallas_call(
        functools.partial(_na_kernel, rows=rows),
        grid=(NA_HEADS, B),
        in_specs=[blk(S, OFF_NA_Q), blk(S, OFF_NA_K), blk(S, OFF_NA_V),
                  blk(L, OFF_NA_K), blk(L, OFF_NA_V),
                  pl.BlockSpec((1,) + tab.shape[1:], lambda h, b: (h, 0, 0, 0))],
        out_specs=pl.BlockSpec((1, S, HEAD_DIM), lambda h, b: (b, 0, h)),
        out_shape=jax.ShapeDtypeStruct((B, S, NA_W), BF16),
        compiler_params=_params("parallel", "parallel"),
        name="na_attention",
    )(z3, z3, z3, zc3, zc3, tab)


def _na_bias_table(rpb, rows):
    col = jnp.arange(GRID_W, dtype=jnp.int32)
    col_start = jnp.clip(col - NA_WIN_COLS // 2, 0, GRID_W - NA_WIN_COLS)
    col_mask = (col[None, :] >= col_start[:, None]) & (col[None, :] < col_start[:, None] + NA_WIN_COLS)
    col_idx = jnp.clip(col[None, :] - col[:, None], -(NA_WIN_COLS - 1), NA_WIN_COLS - 1) + NA_WIN_COLS - 1
    nblk = rows // NA_QROWS
    qrow = NA_QROWS * jnp.arange(nblk)[:, None] + jnp.arange(NA_QROWS)[None, :]
    krow = jnp.array([_na_key_row0(b, rows) for b in range(nblk)])[:, None] + jnp.arange(NA_KROWS)[None, :]
    win0 = jnp.clip(qrow - NA_WIN_ROWS // 2, 0, rows - NA_WIN_ROWS)[:, :, None]
    krow = krow[:, None, :]
    row_ok = (krow >= win0) & (krow < win0 + NA_WIN_ROWS)
    row_idx = jnp.clip(krow - qrow[:, :, None] + NA_WIN_ROWS - 1, 0, 2 * NA_WIN_ROWS - 2)
    t = rpb.astype(F32)[:, row_idx][..., col_idx]
    t = jnp.where(row_ok[None, :, :, :, None, None] & col_mask[None, None, None, None], t, NEG)
    t = t.transpose(0, 1, 2, 4, 3, 5)
    return t.reshape(NA_HEADS, nblk, NA_QROWS * GRID_W, NA_KROWS * GRID_W)


def _rms(x, g):
    return x * lax.rsqrt(jnp.mean(x * x, axis=-1, keepdims=True) + RMS_EPS) * g


def _rope(x, cos, sin_signed):
    lane = lax.broadcasted_iota(jnp.int32, x.shape, 1)
    nxt = pltpu.roll(x, HEAD_DIM - 1, 1)
    prv = pltpu.roll(x, 1, 1)
    partner = jnp.where((lane & 1) == 0, nxt, prv)
    return x * cos + partner * sin_signed


def _gqa_kernel(q0_ref, q1_ref, q2_ref, k_ref, v_ref, kc_ref, vc_ref, cosk_ref, sink_ref,
                cosq_ref, sinq_ref, qn_ref, kn_ref, o_ref, ks, vs, *, S, L):
    @pl.when(pl.program_id(2) == 0)
    def _():
        k = _rope(_rms(k_ref[0].astype(F32), kn_ref[...]), cosk_ref[...], sink_ref[...])
        ks[0:S, :] = k.astype(BF16)
        ks[S:S + L, :] = _rms(kc_ref[0].astype(F32), kn_ref[...]).astype(BF16)
        vs[0:S, :] = v_ref[0]
        vs[S:S + L, :] = vc_ref[0]

    for g, q_ref in enumerate((q0_ref, q1_ref, q2_ref)):
        q = _rope(_rms(q_ref[0].astype(F32), qn_ref[...]), cosq_ref[...], sinq_ref[...]) * ATT_SCALE
        s = _nt_dot(q.astype(BF16), ks[...])
        m = jnp.max(s, axis=-1, keepdims=True)
        p = jnp.exp(s - m)
        l = jnp.sum(p, axis=-1, keepdims=True)
        o = jnp.dot(p.astype(BF16), vs[...], preferred_element_type=F32) / l
        o_ref[0, :, g * HEAD_DIM:(g + 1) * HEAD_DIM] = o.astype(o_ref.dtype)


def _gqa_attention(z3, zc3, cos, sin, qn, kn):
    B, S, _ = z3.shape
    L = zc3.shape[1]
    tq = _tile(S, 512)
    qspec = lambda g: pl.BlockSpec(
        (1, tq, HEAD_DIM), lambda b, h, i: (b, i, OFF_GQ_Q // HEAD_DIM + h * GQ_GROUP + g))
    kv = lambda n, off: pl.BlockSpec((1, n, HEAD_DIM), lambda b, h, i: (b, 0, off // HEAD_DIM + h))
    full = pl.BlockSpec((S, HEAD_DIM), lambda b, h, i: (0, 0))
    tile = pl.BlockSpec((tq, HEAD_DIM), lambda b, h, i: (i, 0))
    vec = pl.BlockSpec((1, HEAD_DIM), lambda b, h, i: (0, 0))
    return pl.pallas_call(
        functools.partial(_gqa_kernel, S=S, L=L),
        grid=(B, GQ_KV_HEADS, S // tq),
        in_specs=[qspec(0), qspec(1), qspec(2), kv(S, OFF_GQ_K), kv(S, OFF_GQ_V),
                  kv(L, OFF_GQ_K), kv(L, OFF_GQ_V), full, full, tile, tile, vec, vec],
        out_specs=pl.BlockSpec((1, tq, GQ_GROUP * HEAD_DIM), lambda b, h, i: (b, i, h)),
        out_shape=jax.ShapeDtypeStruct((B, S, GQ_W), BF16),
        scratch_shapes=[pltpu.VMEM((S + L, HEAD_DIM), BF16), pltpu.VMEM((S + L, HEAD_DIM), BF16)],
        compiler_params=_params("parallel", "parallel", "arbitrary"),
        name="gqa_attention",
    )(z3, z3, z3, z3, z3, zc3, zc3, cos, sin, cos, sin, qn, kn)


def _rope_tables(S):
    t = jnp.arange(S, dtype=jnp.int32)
    row = (t // GRID_W).astype(F32)
    col = (t % GRID_W).astype(F32)
    axis_dim = HEAD_DIM // 2
    inv_freq = 1.0 / (ROPE_THETA ** (jnp.arange(0, axis_dim, 2, dtype=F32) / axis_dim))
    ang = jnp.concatenate([row[:, None] * inv_freq, col[:, None] * inv_freq], axis=-1)
    cos = jnp.repeat(jnp.cos(ang), 2, axis=-1)
    sin = jnp.repeat(jnp.sin(ang), 2, axis=-1)
    sign = jnp.where(jnp.arange(HEAD_DIM) % 2 == 0, -1.0, 1.0).astype(F32)
    return cos, sin * sign


def _ctx_attn_kernel(q_ref, k_ref, v_ref, qn_ref, kn_ref, o_ref, *, norm):
    q = q_ref[0]
    k = k_ref[0]
    if norm:
        q = _rms(q.astype(F32), qn_ref[...]).astype(BF16)
        k = _rms(k.astype(F32), kn_ref[...]).astype(BF16)
    s = _nt_dot(q, k) * ATT_SCALE
    m = jnp.max(s, axis=-1, keepdims=True)
    p = jnp.exp(s - m)
    l = jnp.sum(p, axis=-1, keepdims=True)
    o_ref[0] = (jnp.dot(p.astype(BF16), v_ref[0], preferred_element_type=F32) / l).astype(o_ref.dtype)


def _ctx_attention(zc3, off_q, off_k, off_v, group, norm, qn, kn):
    B, L, _ = zc3.shape
    n_heads = NA_HEADS
    spec = lambda off, div: pl.BlockSpec((1, L, HEAD_DIM), lambda b, h: (b, 0, off // HEAD_DIM + h // div))
    vec = pl.BlockSpec((1, HEAD_DIM), lambda b, h: (0, 0))
    return pl.pallas_call(
        functools.partial(_ctx_attn_kernel, norm=norm),
        grid=(B, n_heads),
        in_specs=[spec(off_q, 1), spec(off_k, group), spec(off_v, group), vec, vec],
        out_specs=pl.BlockSpec((1, L, HEAD_DIM), lambda b, h: (b, 0, h)),
        out_shape=jax.ShapeDtypeStruct((B, L, n_heads * HEAD_DIM), BF16),
        compiler_params=_params("parallel", "parallel"),
        name="ctx_attention",
    )(zc3, zc3, zc3, qn, kn)


SSM_SLABS = SSM_WIDTH // LANES
SLAB_STATE = SSM_LANES // SSM_SLABS


def _s5_kernel(uf_ref, ub_ref, wb_ref, wc_ref, a_ref, h0_ref, yf_ref, yb_ref, hout_ref,
               br, bi, hs, yt, *, tc, n, nb):
    j = pl.program_id(0)
    rows = tc * nb

    @pl.when(j == 0)
    def _():
        hs[...] = h0_ref[...]

    for d, (u_ref, y_ref) in enumerate(((uf_ref, yf_ref), (ub_ref, yb_ref))):
        u = u_ref[...].reshape(rows, SSM_WIDTH).astype(BF16)
        for k in range(SSM_SLABS):
            r = jnp.dot(u[:, k * LANES:(k + 1) * LANES], wb_ref[d, k], preferred_element_type=F32)
            br[:, k * SLAB_STATE:(k + 1) * SLAB_STATE] = r[:, :SLAB_STATE]
            bi[:, k * SLAB_STATE:(k + 1) * SLAB_STATE] = r[:, SLAB_STATE:]
        ar = jnp.broadcast_to(a_ref[d, 0], (nb, SSM_LANES))
        ai = jnp.broadcast_to(a_ref[d, 1], (nb, SSM_LANES))

        def step(i, carry, d=d, ar=ar, ai=ai):
            hr, hi = carry
            t = i if d == 0 else tc - 1 - i
            row = pl.multiple_of(t * nb, nb)
            nr = ar * hr - ai * hi + br[pl.ds(row, nb), :]
            ni = ar * hi + ai * hr + bi[pl.ds(row, nb), :]
            br[pl.ds(row, nb), :] = nr
            bi[pl.ds(row, nb), :] = ni
            return nr, ni

        hr, hi = lax.fori_loop(0, tc, step, (hs[2 * d], hs[2 * d + 1]))
        hs[2 * d] = hr
        hs[2 * d + 1] = hi
        for k in range(SSM_SLABS):
            sl = slice(k * SLAB_STATE, (k + 1) * SLAB_STATE)
            y = (jnp.dot(br[:, sl].astype(BF16), wc_ref[d, k, 0:SLAB_STATE, :], preferred_element_type=F32)
                 + jnp.dot(bi[:, sl].astype(BF16), wc_ref[d, k, SLAB_STATE:, :], preferred_element_type=F32))
            yt[k] = y
        for b in range(nb):
            for k in range(SSM_SLABS):
                y_ref[b, :, k * LANES:(k + 1) * LANES] = yt[k, pl.ds(b, tc, stride=nb), :]

    @pl.when(j == n - 1)
    def _():
        hout_ref[...] = hs[...]


def _s5_scan(u_t, wb, wc, a, h0):
    T, nb, _ = u_t.shape
    tc = _tile(T, 64)
    n = T // tc
    ublk = lambda rev: pl.BlockSpec((tc, nb, SSM_WIDTH), (lambda j: (n - 1 - j, 0, 0)) if rev else (lambda j: (j, 0, 0)))
    yblk = lambda rev: pl.BlockSpec((nb, tc, SSM_WIDTH), (lambda j: (0, n - 1 - j, 0)) if rev else (lambda j: (0, j, 0)))
    const = lambda shape: pl.BlockSpec(shape, lambda j: (0,) * len(shape))
    return pl.pallas_call(
        functools.partial(_s5_kernel, tc=tc, n=n, nb=nb),
        grid=(n,),
        in_specs=[ublk(False), ublk(True), const(wb.shape), const(wc.shape), const(a.shape), const(h0.shape)],
        out_specs=[yblk(False), yblk(True), const(h0.shape)],
        out_shape=[jax.ShapeDtypeStruct((nb, T, SSM_WIDTH), F32), jax.ShapeDtypeStruct((nb, T, SSM_WIDTH), F32),
                   jax.ShapeDtypeStruct(h0.shape, F32)],
        scratch_shapes=[pltpu.VMEM((tc * nb, SSM_LANES), F32), pltpu.VMEM((tc * nb, SSM_LANES), F32),
                        pltpu.VMEM(h0.shape, F32), pltpu.VMEM((SSM_SLABS, tc * nb, LANES), F32)],
        compiler_params=_params("arbitrary"),
        name="s5_scan",
    )(u_t, u_t, wb, wc, a, h0)


def _s5_weights(lam_r, lam_i, log_dt, b_r, b_i, c_r, c_i):
    lam_r, lam_i, log_dt = lam_r.astype(F32), lam_i.astype(F32), log_dt.astype(F32)
    dt = jnp.exp(log_dt)[..., None]
    mag = jnp.exp(lam_r * dt)
    ar, ai = mag * jnp.cos(lam_i * dt), mag * jnp.sin(lam_i * dt)
    den = lam_r * lam_r + lam_i * lam_i
    nr, ni = ar - 1.0, ai
    kr, ki = (nr * lam_r + ni * lam_i) / den, (ni * lam_r - nr * lam_i) / den
    b_r, b_i = b_r.astype(F32), b_i.astype(F32)
    bbr = kr[..., None] * b_r - ki[..., None] * b_i
    bbi = kr[..., None] * b_i + ki[..., None] * b_r
    gs = SSM_GROUPS // SSM_SLABS
    eye = jnp.eye(gs, dtype=F32)

    def in_slab(w):
        w = w.reshape(2, SSM_SLABS, gs, SSM_STATE, SSM_GROUP)
        return jnp.einsum('dkgph,gj->dkghjp', w, eye).reshape(2, SSM_SLABS, gs * SSM_GROUP, gs * SSM_STATE)

    def out_slab(w):
        w = w.reshape(2, SSM_SLABS, gs, SSM_GROUP, SSM_STATE)
        return jnp.einsum('dkghp,gj->dkjpgh', w, eye).reshape(2, SSM_SLABS, gs * SSM_STATE, gs * SSM_GROUP)

    wb = jnp.concatenate([in_slab(bbr), in_slab(bbi)], axis=-1).astype(BF16)
    wc = jnp.concatenate([out_slab(c_r.astype(F32)), -out_slab(c_i.astype(F32))], axis=-2).astype(BF16)
    a = jnp.stack([ar.reshape(2, 1, SSM_LANES), ai.reshape(2, 1, SSM_LANES)], axis=1)
    return wb, wc, a


def _gelu_tanh(x):
    return 0.5 * x * (1.0 + jnp.tanh(math.sqrt(2.0 / math.pi) * (x + 0.044715 * (x * x * x))))


def _glu_kernel(u_ref, yf_ref, yb_ref, d_ref, w_ref, b_ref, o_ref):
    y = u_ref[0].astype(F32) * d_ref[...] + yf_ref[...] + yb_ref[...]
    t = _gelu_tanh(y).astype(BF16)
    r = jnp.dot(t, w_ref[...], preferred_element_type=F32) + b_ref[...]
    o_ref[0] = (r[:, :SSM_WIDTH] * _sigmoid(r[:, SSM_WIDTH:])).astype(o_ref.dtype)


def _glu(z3, yf, yb, d, w, b):
    B, S, _ = z3.shape
    ts = _tile(S, 512)
    return pl.pallas_call(
        _glu_kernel,
        grid=(B, S // ts),
        in_specs=[pl.BlockSpec((1, ts, SSM_WIDTH), lambda b, i: (b, i, OFF_SU // SSM_WIDTH)),
                  pl.BlockSpec((None, ts, SSM_WIDTH), lambda b, i: (b, i, 0)),
                  pl.BlockSpec((None, ts, SSM_WIDTH), lambda b, i: (b, i, 0)),
                  pl.BlockSpec((1, SSM_WIDTH), lambda b, i: (0, 0)),
                  pl.BlockSpec((SSM_WIDTH, 2 * SSM_WIDTH), lambda b, i: (0, 0)),
                  pl.BlockSpec((1, 2 * SSM_WIDTH), lambda b, i: (0, 0))],
        out_specs=pl.BlockSpec((1, ts, SSM_WIDTH), lambda b, i: (b, i, 0)),
        out_shape=jax.ShapeDtypeStruct((B, S, SSM_WIDTH), BF16),
        compiler_params=_params("parallel", "parallel"),
        name="s5_glu",
    )(z3, yf, yb, d, w, b)


def _merge_kernel(a_ref, b_ref, s_ref, ga_ref, gb_ref, gs_ref, wa_ref, wb_ref, ws_ref, o_ref):
    m = (_sigmoid(ga_ref[...].astype(F32)) * jnp.dot(a_ref[...], wa_ref[...], preferred_element_type=F32)
         + _sigmoid(gb_ref[...].astype(F32)) * jnp.dot(b_ref[...], wb_ref[...], preferred_element_type=F32)
         + _sigmoid(gs_ref[...].astype(F32)) * jnp.dot(s_ref[...], ws_ref[...], preferred_element_type=F32))
    o_ref[...] = m.astype(o_ref.dtype)


def _merge(a, b, s, z2, wa, wb, ws):
    M = a.shape[0]
    D = wa.shape[1]
    tm, tn = _tile(M, 1024), _tile(D, 1024)
    row = lambda w: pl.BlockSpec((tm, w), lambda i, j: (i, 0))
    gate = lambda br: pl.BlockSpec((tm, tn), lambda i, j: (i, (OFF_GATE + br * D) // tn + j))
    wsp = lambda w: pl.BlockSpec((w, tn), lambda i, j: (0, j))
    return pl.pallas_call(
        _merge_kernel,
        grid=(M // tm, D // tn),
        in_specs=[row(NA_W), row(GQ_W), row(SSM_WIDTH), gate(0), gate(1), gate(2),
                  wsp(NA_W), wsp(GQ_W), wsp(SSM_WIDTH)],
        out_specs=pl.BlockSpec((tm, tn), lambda i, j: (i, j)),
        out_shape=jax.ShapeDtypeStruct((M, D), BF16),
        compiler_params=_params("parallel", "arbitrary"),
        name="merge",
    )(a, b, s, z2, z2, z2, wa, wb, ws)


def _layer_norm(x, g, b):
    mu = jnp.mean(x, axis=-1, keepdims=True)
    xc = x - mu
    var = jnp.mean(xc * xc, axis=-1, keepdims=True)
    return xc * lax.rsqrt(var + LN_EPS) * g + b


def _outproj_kernel(m_ref, w_ref, h_ref, g1_ref, sh_ref, sc_ref, lng_ref, lnb_ref, rwh_ref, rwl_ref, rb_ref,
                    cnt0_ref, hn_ref, tokr_ref, tokb_ref, idx_ref, wts_ref, rank_ref, cnt_ref, cnt, *, tm, D):
    @pl.when(pl.program_id(0) == 0)
    def _():
        cnt[...] = cnt0_ref[...]

    mix = jnp.dot(m_ref[...], w_ref[...], preferred_element_type=F32)
    hn = _layer_norm(DN_ALPHA * h_ref[...] + g1_ref[0] * mix, lng_ref[...], lnb_ref[...])
    hn_ref[...] = hn
    tok = hn * (1.0 + sc_ref[0]) + sh_ref[0]
    tokr_ref[...] = tok
    t_hi = tok.astype(BF16)
    tokb_ref[...] = t_hi
    t_lo = (tok - t_hi.astype(F32)).astype(BF16)
    logits = (jnp.dot(t_hi, rwh_ref[...], preferred_element_type=F32)
              + jnp.dot(t_lo, rwh_ref[...], preferred_element_type=F32)
              + jnp.dot(t_hi, rwl_ref[...], preferred_element_type=F32))
    scores = _sigmoid(logits)
    sel = scores + rb_ref[...]
    lane = lax.broadcasted_iota(jnp.int32, sel.shape, 1).astype(F32)
    slot = lax.broadcasted_iota(jnp.int32, (tm, LANES), 1)
    idx_acc = jnp.zeros((tm, LANES), F32)
    w_acc = jnp.zeros((tm, LANES), F32)
    hits = []
    for k in range(TOP_K):
        mx = jnp.max(sel, axis=-1, keepdims=True)
        am = jnp.min(jnp.where(sel == mx, lane, float(N_EXPERTS)), axis=-1, keepdims=True)
        hit = lane == am
        hits.append(hit)
        wk = jnp.sum(jnp.where(hit, scores, 0.0), axis=-1, keepdims=True)
        idx_acc = jnp.where(slot == k, am, idx_acc)
        w_acc = jnp.where(slot == k, wk, w_acc)
        sel = jnp.where(hit, -jnp.inf, sel)
    wsum = jnp.sum(w_acc, axis=-1, keepdims=True)
    idx_ref[...] = idx_acc.astype(jnp.int32)
    wts_ref[...] = w_acc / wsum * ROUTED_SCALE

    picked = jnp.zeros(sel.shape, F32)
    for hit in hits:
        picked = jnp.where(hit, 1.0, picked)
    ri = lax.broadcasted_iota(jnp.int32, (tm, tm), 0)
    ci = lax.broadcasted_iota(jnp.int32, (tm, tm), 1)
    below = jnp.where(ci < ri, 1.0, 0.0).astype(BF16)
    before = jnp.dot(below, picked.astype(BF16), preferred_element_type=F32) + cnt[...]
    rank_acc = jnp.zeros((tm, LANES), F32)
    for k, hit in enumerate(hits):
        rk = jnp.sum(jnp.where(hit, before, 0.0), axis=-1, keepdims=True)
        rank_acc = jnp.where(slot == k, rk, rank_acc)
    rank_ref[...] = rank_acc.astype(jnp.int32)
    cnt[...] = cnt[...] + jnp.sum(picked, axis=0, keepdims=True)
    cnt_ref[...] = cnt[...]


def _outproj_ln_route(m, w_out, h2, mod3, row_fn, ln_g, ln_b, rw_hi, rw_lo, rbias, cnt0, tm=256):
    M, D = m.shape
    tm = _tile(M, tm)
    vec = lambda k: pl.BlockSpec((1, 1, D), lambda i: (row_fn(i), 0, k))
    const = lambda shape: pl.BlockSpec(shape, lambda i: (0,) * len(shape))
    rowblk = lambda w: pl.BlockSpec((tm, w), lambda i: (i, 0))
    return pl.pallas_call(
        functools.partial(_outproj_kernel, tm=tm, D=D),
        grid=(M // tm,),
        in_specs=[rowblk(D), const((D, D)), rowblk(D), vec(2), vec(3), vec(4), const((1, D)), const((1, D)),
                  const((D, N_EXPERTS)), const((D, N_EXPERTS)), const((1, N_EXPERTS)), const((1, N_EXPERTS))],
        out_specs=[rowblk(D), rowblk(D), rowblk(D),
                   rowblk(LANES), rowblk(LANES), rowblk(LANES), const((1, N_EXPERTS))],
        out_shape=[jax.ShapeDtypeStruct((M, D), F32), jax.ShapeDtypeStruct((M, D), F32),
                   jax.ShapeDtypeStruct((M, D), BF16), jax.ShapeDtypeStruct((M, LANES), jnp.int32),
                   jax.ShapeDtypeStruct((M, LANES), F32), jax.ShapeDtypeStruct((M, LANES), jnp.int32),
                   jax.ShapeDtypeStruct((1, N_EXPERTS), F32)],
        scratch_shapes=[pltpu.VMEM((1, N_EXPERTS), F32)],
        compiler_params=_params("arbitrary"),
        name="outproj_ln_route",
    )(m, w_out, h2, mod3, mod3, mod3, ln_g, ln_b, rw_hi, rw_lo, rbias, cnt0)


def _swiglu_kernel(x_ref, w13_ref, w2_ref, o_ref, *, E):
    h = jnp.dot(x_ref[...], w13_ref[...], preferred_element_type=F32)
    g = h[:, :E]
    act = (g * _sigmoid(g) * h[:, E:]).astype(BF16)
    o_ref[...] = jnp.dot(act, w2_ref[...], preferred_element_type=F32).astype(o_ref.dtype)


def _shared_expert(tokb, w13, w2):
    T, D = tokb.shape
    E = w2.shape[0]
    tm = _tile(T, 512)
    return pl.pallas_call(
        functools.partial(_swiglu_kernel, E=E),
        grid=(T // tm,),
        in_specs=[pl.BlockSpec((tm, D), lambda i: (i, 0)), pl.BlockSpec((D, 2 * E), lambda i: (0, 0)),
                  pl.BlockSpec((E, D), lambda i: (0, 0))],
        out_specs=pl.BlockSpec((tm, D), lambda i: (i, 0)),
        out_shape=jax.ShapeDtypeStruct((T, D), BF16),
        compiler_params=_params("parallel"),
        name="shared_expert",
    )(tokb, w13, w2)


def _route_plan(idx, rank, counts, tm):
    T = idx.shape[0]
    counts = counts.reshape(N_EXPERTS).astype(jnp.int32)
    padded = (counts + tm - 1) // tm * tm
    pend = jnp.cumsum(padded)
    pstart = pend - padded
    experts = jnp.arange(N_EXPERTS, dtype=jnp.int32)
    dest = (jnp.sum(jnp.where(idx[..., None] == experts, pstart, 0), axis=-1) + rank).astype(jnp.int32)
    n_blk = -(-(T * TOP_K + N_EXPERTS * (tm - 1)) // tm)
    first = jnp.arange(n_blk, dtype=jnp.int32) * tm
    blk_exp = jnp.minimum(jnp.sum(first[:, None] >= pend[None, :], axis=1), N_EXPERTS - 1).astype(jnp.int32)
    n_used = (pend[-1:] // tm).astype(jnp.int32)
    return dest, blk_exp, n_used, (pstart + counts).astype(jnp.int32), (padded - counts).astype(jnp.int32)


def _dispatch_kernel(ps_ref, pc_ref, nu_ref, dest_ref, tok_ref, xs_hbm, zbuf, sem, *, tb, n, tm, n_blk):
    i = pl.program_id(0)

    def pad_rows(do):
        def per_expert(e, c):
            def per_row(r, c2):
                do(pltpu.make_async_copy(zbuf.at[pl.ds(0, 1)], xs_hbm.at[pl.ds(ps_ref[e] + r, 1)], sem.at[1]))
                return c2
            return lax.fori_loop(0, pc_ref[e], per_row, c)
        lax.fori_loop(0, N_EXPERTS, per_expert, 0)

        def per_block(b, c):
            do(pltpu.make_async_copy(zbuf, xs_hbm.at[pl.ds(pl.multiple_of(b * tm, tm), tm)], sem.at[1]))
            return c
        lax.fori_loop(nu_ref[0], n_blk, per_block, 0)

    @pl.when(i == 0)
    def _():
        zbuf[...] = jnp.zeros(zbuf.shape, zbuf.dtype)
        pad_rows(lambda cp: cp.start())

    def body(t, c):
        src = tok_ref.at[pl.ds(t, 1)]
        for k in range(TOP_K):
            d = dest_ref[0, 0, t * TOP_K + k]
            pltpu.make_async_copy(src, xs_hbm.at[pl.ds(d, 1)], sem.at[0]).start()
        return c
    lax.fori_loop(0, tb, body, 0)
    for k in range(TOP_K):
        pltpu.make_async_copy(tok_ref, xs_hbm.at[pl.ds(0, tb)], sem.at[0]).wait()

    @pl.when(i == n - 1)
    def _():
        pad_rows(lambda cp: cp.wait())


def _dispatch(tokr, dest, pad_start, pad_cnt, n_used, n_blk, tm, tb=256):
    T, D = tokr.shape
    tb = _tile(T, tb)
    n = T // tb
    grid_spec = pltpu.PrefetchScalarGridSpec(
        num_scalar_prefetch=3,
        grid=(n,),
        in_specs=[pl.BlockSpec((1, 1, tb * TOP_K), lambda i, ps, pc, nu: (i, 0, 0), memory_space=pltpu.SMEM),
                  pl.BlockSpec((tb, D), lambda i, ps, pc, nu: (i, 0))],
        out_specs=pl.BlockSpec(memory_space=pl.ANY),
        scratch_shapes=[pltpu.VMEM((tm, D), F32), pltpu.SemaphoreType.DMA((2,))],
    )
    return pl.pallas_call(
        functools.partial(_dispatch_kernel, tb=tb, n=n, tm=tm, n_blk=n_blk),
        grid_spec=grid_spec,
        out_shape=jax.ShapeDtypeStruct((n_blk * tm, D), F32),
        compiler_params=pltpu.CompilerParams(dimension_semantics=("arbitrary",), vmem_limit_bytes=VMEM_LIMIT,
                                             disable_bounds_checks=True),
        name="moe_dispatch",
    )(pad_start, pad_cnt, n_used, dest.reshape(n, 1, tb * TOP_K), tokr)


def _expert_kernel(be_ref, nu_ref, x_ref, w1_ref, w3_ref, w2_ref, y_ref, w13s, w2s, *, E):
    i = pl.program_id(0)

    @pl.when((i == 0) | (be_ref[i] != be_ref[jnp.maximum(i - 1, 0)]))
    def _():
        w13s[:, 0:E] = w1_ref[...].astype(BF16)
        w13s[:, E:2 * E] = w3_ref[...].astype(BF16)
        w2s[...] = w2_ref[...].astype(BF16)

    @pl.when(i < nu_ref[0])
    def _():
        h = jnp.dot(x_ref[...].astype(BF16), w13s[...], preferred_element_type=F32)
        g = h[:, :E]
        act = (g * _sigmoid(g) * h[:, E:]).astype(BF16)
        y_ref[...] = jnp.dot(act, w2s[...], preferred_element_type=F32)

    @pl.when(i >= nu_ref[0])
    def _():
        y_ref[...] = jnp.zeros(y_ref.shape, y_ref.dtype)


def _routed_experts(xs, blk_exp, n_used, w1, w3, w2, l, tm):
    n = blk_exp.shape[0]
    _, _, D, E = w1.shape
    wspec = lambda a, b: pl.BlockSpec((None, None, a, b), lambda i, be, nu: (l, be[i], 0, 0))
    grid_spec = pltpu.PrefetchScalarGridSpec(
        num_scalar_prefetch=2,
        grid=(n,),
        in_specs=[pl.BlockSpec((tm, D), lambda i, be, nu: (jnp.where(i < nu[0], i, 0), 0)),
                  wspec(D, E), wspec(D, E), wspec(E, D)],
        out_specs=pl.BlockSpec((tm, D), lambda i, be, nu: (i, 0)),
        scratch_shapes=[pltpu.VMEM((D, 2 * E), BF16), pltpu.VMEM((E, D), BF16)],
    )
    return pl.pallas_call(
        functools.partial(_expert_kernel, E=E),
        grid_spec=grid_spec,
        out_shape=jax.ShapeDtypeStruct((n * tm, D), F32),
        compiler_params=_params("arbitrary"),
        name="routed_experts",
    )(blk_exp, n_used, xs, w1, w3, w2)


def _combine_kernel(dcur_ref, dnxt_ref, h_ref, sh_ref, w_ref, g2_ref, lng_ref, lnb_ref, ys_hbm, o_ref,
                    ybuf, sem, *, tb, D, n):
    i = pl.program_id(0)
    rows = tb * TOP_K

    def gather(slot, d_ref):
        def body(g, c):
            t0 = pl.multiple_of(g * SUBLANES, SUBLANES)
            dst0 = pl.multiple_of(slot * rows + t0, SUBLANES)
            for u in range(SUBLANES):
                for k in range(TOP_K):
                    d = d_ref[0, 0, (t0 + u) * TOP_K + k]
                    pltpu.make_async_copy(ys_hbm.at[pl.ds(d, 1)], ybuf.at[pl.ds(dst0 + (k * tb + u), 1)],
                                          sem.at[slot]).start()
            return c
        lax.fori_loop(0, tb // SUBLANES, body, 0)

    @pl.when(i == 0)
    def _():
        gather(0, dcur_ref)

    @pl.when(i + 1 < n)
    def _():
        gather((i + 1) % 2, dnxt_ref)

    slot = i % 2
    base = slot * rows
    pltpu.make_async_copy(ys_hbm.at[pl.ds(0, rows)], ybuf.at[pl.ds(pl.multiple_of(base, rows), rows)],
                          sem.at[slot]).wait()
    w = w_ref[...]
    f = sh_ref[...].astype(F32)
    for k in range(TOP_K):
        f = f + w[:, k:k + 1] * ybuf[pl.ds(pl.multiple_of(base + k * tb, tb), tb), :]
    o_ref[...] = _layer_norm(DN_ALPHA * h_ref[...] + g2_ref[0] * f, lng_ref[...], lnb_ref[...])


def _combine_ln(h2, shared, wts, ys, dest, tok_off, mod3, row_fn, ln_g, ln_b, tb=128):
    M, D = h2.shape
    tb = _tile(M, tb)
    n = M // tb
    off = tok_off // tb
    dest3 = dest.reshape(-1, 1, tb * TOP_K)
    rowblk = lambda w, o: pl.BlockSpec((tb, w), lambda i: (i + o, 0))
    const = lambda shape: pl.BlockSpec(shape, lambda i: (0,) * len(shape))
    smem = lambda fn: pl.BlockSpec((1, 1, tb * TOP_K), fn, memory_space=pltpu.SMEM)
    return pl.pallas_call(
        functools.partial(_combine_kernel, tb=tb, D=D, n=n),
        grid=(n,),
        in_specs=[smem(lambda i: (i + off, 0, 0)), smem(lambda i: (jnp.minimum(i + 1, n - 1) + off, 0, 0)),
                  rowblk(D, 0), rowblk(D, off), rowblk(LANES, off),
                  pl.BlockSpec((1, 1, D), lambda i: (row_fn(i), 0, 5)), const((1, D)), const((1, D)),
                  pl.BlockSpec(memory_space=pl.ANY)],
        out_specs=rowblk(D, 0),
        out_shape=jax.ShapeDtypeStruct((M, D), F32),
        scratch_shapes=[pltpu.VMEM((2 * tb * TOP_K, D), F32), pltpu.SemaphoreType.DMA((2,))],
        compiler_params=pltpu.CompilerParams(dimension_semantics=("arbitrary",), vmem_limit_bytes=VMEM_LIMIT,
                                             disable_bounds_checks=True),
        name="combine_ln",
    )(dest3, dest3, h2, shared, wts, mod3, ln_g, ln_b, ys)


MOE_TM = 256


def kernel(x, c, ctx, c_ctx, w_mod, b_mod, w_in, na_rpb, gq_q_norm, gq_k_norm, ssm_lam_re, ssm_lam_im, ssm_log_dt, ssm_b_re, ssm_b_im, ssm_c_re, ssm_c_im, ssm_d, ssm_glu_w, ssm_glu_b, w_br_na, w_br_gq, w_br_ssm, w_out, ln1_g, ln1_b, router_w, router_bias, exp_w1, exp_w3, exp_w2, sh_w1, sh_w3, sh_w2, ln2_g, ln2_b):
    B, S, D = x.shape
    L = ctx.shape[1]
    assert B < 2 * SUBLANES and D % LANES == 0
    cos, sin = _rope_tables(S)
    c_all = jnp.zeros((2 * SUBLANES, D), F32).at[:B].set(c).at[B].set(c_ctx)
    lat_row = lambda b: b
    ctx_row = lambda b: B
    h, hc = x, ctx
    for l in range(DEPTH):
        need_ctx = l < DEPTH - 1
        mod3 = _modvec(c_all, w_mod, l, b_mod[l])[:, None, :]
        vec = lambda p: p[l].reshape(1, -1).astype(F32)

        w_in_b = w_in[l].astype(BF16)
        u = _modulate(h, mod3, lat_row, 0, 1)
        uc = _modulate(hc, mod3, ctx_row, 0, 1)
        z2 = _matmul(u.reshape(B * S, D), w_in_b, BF16)
        zc2 = _matmul(uc.reshape(B * L, D), w_in_b if need_ctx else w_in_b[:, :CTX_IN_W], BF16)
        z3, zc3 = z2.reshape(B, S, -1), zc2.reshape(B, L, -1)

        a_out = _na_attention(z3, zc3, _na_bias_table(na_rpb[l], S // GRID_W))
        qn, kn = vec(gq_q_norm), vec(gq_k_norm)
        b_out = _gqa_attention(z3, zc3, cos, sin, qn, kn)

        wb, wc, a = _s5_weights(ssm_lam_re[l], ssm_lam_im[l], ssm_log_dt[l], ssm_b_re[l], ssm_b_im[l],
                                ssm_c_re[l], ssm_c_im[l])
        su = lambda t: t[:, :, OFF_SU:OFF_SU + SSM_WIDTH].astype(F32).transpose(1, 0, 2)
        h0 = jnp.zeros((4, B, SSM_LANES), F32)
        ycf, ycb, hfin = _s5_scan(su(zc3), wb, wc, a, h0)
        yf, yb, _ = _s5_scan(su(z3), wb, wc, a, hfin)
        glu_w, glu_b, dvec = ssm_glu_w[l].astype(BF16), vec(ssm_glu_b), vec(ssm_d)
        c_out = _glu(z3, yf, yb, dvec, glu_w, glu_b)

        wa, wg, ws, wo = (w[l].astype(BF16) for w in (w_br_na, w_br_gq, w_br_ssm, w_out))
        m = _merge(a_out.reshape(B * S, -1), b_out.reshape(B * S, -1), c_out.reshape(B * S, -1), z2, wa, wg, ws)
        rw = router_w[l].astype(F32)
        rw_hi = rw.astype(BF16)
        rw_lo = (rw - rw_hi.astype(F32)).astype(BF16)
        rbias = vec(router_bias)
        g1, b1 = vec(ln1_g), vec(ln1_b)
        tm_o = _tile(S, 256)
        h2, tokr, tokb, idx, wts, rank, counts = _outproj_ln_route(
            m, wo, h.reshape(B * S, D), mod3, lambda i: i // (S // tm_o), g1, b1, rw_hi, rw_lo, rbias,
            jnp.zeros((1, N_EXPERTS), F32), tm=tm_o)
        if need_ctx:
            a_ctx = _ctx_attention(zc3, OFF_NA_Q, OFF_NA_K, OFF_NA_V, 1, False, qn, kn)
            b_ctx = _ctx_attention(zc3, OFF_GQ_Q, OFF_GQ_K, OFF_GQ_V, GQ_GROUP, True, qn, kn)
            c_ctx_out = _glu(zc3, ycf, ycb, dvec, glu_w, glu_b)
            mc = _merge(a_ctx.reshape(B * L, -1), b_ctx.reshape(B * L, -1), c_ctx_out.reshape(B * L, -1),
                        zc2, wa, wg, ws)
            tm_c = _tile(L, 256)
            hc2, tokr_c, tokb_c, idx_c, wts_c, rank_c, counts = _outproj_ln_route(
                mc, wo, hc.reshape(B * L, D), mod3, lambda i: B, g1, b1, rw_hi, rw_lo, rbias, counts, tm=tm_c)
            tokr = jnp.concatenate([tokr, tokr_c], axis=0)
            tokb = jnp.concatenate([tokb, tokb_c], axis=0)
            idx = jnp.concatenate([idx, idx_c], axis=0)
            wts = jnp.concatenate([wts, wts_c], axis=0)
            rank = jnp.concatenate([rank, rank_c], axis=0)

        w13 = jnp.concatenate([sh_w1[l], sh_w3[l]], axis=1).astype(BF16)
        shared = _shared_expert(tokb, w13, sh_w2[l].astype(BF16))
        dest, blk_exp, n_used, pad_start, pad_cnt = _route_plan(idx[:, :TOP_K], rank[:, :TOP_K], counts, MOE_TM)
        xs = _dispatch(tokr, dest, pad_start, pad_cnt, n_used, blk_exp.shape[0], MOE_TM)
        ys = _routed_experts(xs, blk_exp, n_used, exp_w1, exp_w3, exp_w2, l, MOE_TM)
        g2, b2 = vec(ln2_g), vec(ln2_b)
        tb_l = _tile(S, 128)
        h = _combine_ln(h2, shared, wts, ys, dest, 0, mod3, lambda i: i // (S // tb_l), g2, b2,
                        tb=tb_l).reshape(B, S, D)
        if need_ctx:
            hc = _combine_ln(hc2, shared, wts, ys, dest, B * S, mod3, lambda i: B, g2, b2,
                             tb=_tile(L, 128)).reshape(B, L, D)
    return h
```

```python
import functools
import math

import jax
import jax.numpy as jnp
import numpy as np
from jax import lax
from jax.experimental import pallas as pl
from jax.experimental.pallas import tpu as pltpu

F32 = jnp.float32
BF16 = jnp.bfloat16

DEPTH = 2
GRID_W = 64
HEAD_DIM = 128
NA_HEADS = 6
NA_WIN_ROWS = 8
NA_WIN_COLS = 16
GQ_HEADS = 6
GQ_KV_HEADS = 2
GQ_GROUP = GQ_HEADS // GQ_KV_HEADS
ROPE_THETA = 10000.0
SSM_GROUP = 16
SSM_WIDTH = 512
SSM_GROUPS = SSM_WIDTH // SSM_GROUP
SSM_STATE = 64
SSM_LANES = SSM_GROUPS * SSM_STATE
N_EXPERTS = 64
TOP_K = 8
EXPERT_DIM = 512
ROUTED_SCALE = 2.5
DN_ALPHA = (2 * DEPTH) ** 0.25
LN_EPS = 1e-6
RMS_EPS = 1e-6
ATT_SCALE = HEAD_DIM ** -0.5
NEG = -1e30
NA_QROWS = 8
NA_KROWS = 16

LANES = 128
SUBLANES = 8
VMEM_LIMIT = 56 * 1024 * 1024

NA_W = NA_HEADS * HEAD_DIM
GQ_W = GQ_HEADS * HEAD_DIM
GQ_KV_W = GQ_KV_HEADS * HEAD_DIM
OFF_NA_K = 0
OFF_NA_V = OFF_NA_K + NA_W
OFF_GQ_K = OFF_NA_V + NA_W
OFF_GQ_V = OFF_GQ_K + GQ_KV_W
OFF_SU = OFF_GQ_V + GQ_KV_W
CTX_IN_W = OFF_SU + SSM_WIDTH
OFF_NA_Q = CTX_IN_W
OFF_GQ_Q = OFF_NA_Q + NA_W
OFF_GATE = OFF_GQ_Q + GQ_W


def _tile(n, pref):
    t = min(n, pref)
    while n % t:
        t //= 2
    return t


def _params(*sem):
    return pltpu.CompilerParams(dimension_semantics=sem, vmem_limit_bytes=VMEM_LIMIT)


def _sigmoid(x):
    return 1.0 / (1.0 + jnp.exp(-x))


def _modvec_kernel(c_ref, w_ref, b_ref, o_ref):
    c = c_ref[...]
    a = (c * _sigmoid(c)).astype(BF16)
    o_ref[...] = jnp.dot(a, w_ref[...].astype(BF16), preferred_element_type=F32) + b_ref[...]


def _modvec(c_all, w, l, b):
    R, D = c_all.shape
    N = w.shape[2]
    tn = _tile(N, 1024)
    return pl.pallas_call(
        _modvec_kernel,
        grid=(N // tn,),
        in_specs=[pl.BlockSpec((R, D), lambda j: (0, 0)),
                  pl.BlockSpec((None, D, tn), lambda j: (l, 0, j)),
                  pl.BlockSpec((1, tn), lambda j: (0, j))],
        out_specs=pl.BlockSpec((R, tn), lambda j: (0, j)),
        out_shape=jax.ShapeDtypeStruct((R, N), F32),
        compiler_params=_params("arbitrary"),
        name="modvec",
    )(c_all, w, b.reshape(1, N))


def _modulate_kernel(h_ref, sh_ref, sc_ref, o_ref):
    o_ref[0] = (h_ref[0] * (1.0 + sc_ref[0]) + sh_ref[0]).astype(o_ref.dtype)


def _modulate(h, mod3, row_fn, k_shift, k_scale):
    B, S, D = h.shape
    ts = _tile(S, 512)
    return pl.pallas_call(
        _modulate_kernel,
        grid=(B, S // ts),
        in_specs=[pl.BlockSpec((1, ts, D), lambda b, i: (b, i, 0)),
                  pl.BlockSpec((1, 1, D), lambda b, i: (row_fn(b), 0, k_shift)),
                  pl.BlockSpec((1, 1, D), lambda b, i: (row_fn(b), 0, k_scale))],
        out_specs=pl.BlockSpec((1, ts, D), lambda b, i: (b, i, 0)),
        out_shape=jax.ShapeDtypeStruct((B, S, D), BF16),
        compiler_params=_params("parallel", "parallel"),
        name="modulate",
    )(h, mod3, mod3)


def _mm_kernel(x_ref, w_ref, o_ref):
    o_ref[...] = jnp.dot(x_ref[...], w_ref[...], preferred_element_type=F32).astype(o_ref.dtype)


def _matmul(x, w, out_dtype, tm=1024, tn=1024):
    M, K = x.shape
    N = w.shape[1]
    tm, tn = _tile(M, tm), _tile(N, tn)
    return pl.pallas_call(
        _mm_kernel,
        grid=(M // tm, N // tn),
        in_specs=[pl.BlockSpec((tm, K), lambda i, j: (i, 0)),
                  pl.BlockSpec((K, tn), lambda i, j: (0, j))],
        out_specs=pl.BlockSpec((tm, tn), lambda i, j: (i, j)),
        out_shape=jax.ShapeDtypeStruct((M, N), out_dtype),
        compiler_params=_params("parallel", "arbitrary"),
        name="matmul",
    )(x, w)


def _nt_dot(a, b):
    return lax.dot_general(a, b, (((1,), (1,)), ((), ())), preferred_element_type=F32)


def _na_key_row0(blk, rows):
    return min(max(NA_QROWS * blk - NA_WIN_ROWS // 2, 0), rows - NA_KROWS)


def _na_tile_index(d):
    return min(max(d + NA_WIN_ROWS - 1, -1), 2 * NA_WIN_ROWS - 2) + 1


def _na_kernel(q_ref, k_ref, v_ref, kc_ref, vc_ref, tile_ref, mask_ref, o_ref, *, rows):
    kc = kc_ref[0]
    vc = vc_ref[0]
    nq = NA_QROWS * GRID_W
    for blk in range(rows // NA_QROWS):
        row0 = _na_key_row0(blk, rows)
        k0 = row0 * GRID_W
        q = q_ref[0, blk * nq:(blk + 1) * nq, :]
        kb = k_ref[0, k0:k0 + NA_KROWS * GRID_W, :]
        vb = v_ref[0, k0:k0 + NA_KROWS * GRID_W, :]
        bias = jnp.concatenate([
            jnp.concatenate([tile_ref[0, _na_tile_index(row0 + 2 * jp - (NA_QROWS * blk + a))]
                             for jp in range(NA_KROWS // 2)], axis=1)
            for a in range(NA_QROWS)], axis=0)
        s1 = _nt_dot(q, kb) * ATT_SCALE + bias + mask_ref[blk]
        s2 = _nt_dot(q, kc) * ATT_SCALE
        m = jnp.maximum(jnp.max(s1, axis=-1, keepdims=True), jnp.max(s2, axis=-1, keepdims=True))
        p1 = jnp.exp(s1 - m)
        p2 = jnp.exp(s2 - m)
        l = jnp.sum(p1, axis=-1, keepdims=True) + jnp.sum(p2, axis=-1, keepdims=True)
        o = (jnp.dot(p1.astype(BF16), vb, preferred_element_type=F32)
             + jnp.dot(p2.astype(BF16), vc, preferred_element_type=F32))
        o_ref[0, blk * nq:(blk + 1) * nq, :] = (o / l).astype(o_ref.dtype)


def _na_attention(z3, zc3, tiles, mask):
    B, S, _ = z3.shape
    L = zc3.shape[1]
    rows = S // GRID_W
    assert rows >= NA_KROWS and rows % NA_QROWS == 0
    blk = lambda n, off: pl.BlockSpec((1, n, HEAD_DIM), lambda h, b: (b, 0, off // HEAD_DIM + h))
    return pl.pallas_call(
        functools.partial(_na_kernel, rows=rows),
        grid=(NA_HEADS, B),
        in_specs=[blk(S, OFF_NA_Q), blk(S, OFF_NA_K), blk(S, OFF_NA_V),
                  blk(L, OFF_NA_K), blk(L, OFF_NA_V),
                  pl.BlockSpec((1,) + tiles.shape[1:], lambda h, b: (h, 0, 0, 0)),
                  pl.BlockSpec(mask.shape, lambda h, b: (0, 0, 0))],
        out_specs=pl.BlockSpec((1, S, HEAD_DIM), lambda h, b: (b, 0, h)),
        out_shape=jax.ShapeDtypeStruct((B, S, NA_W), BF16),
        compiler_params=_params("parallel", "parallel"),
        name="na_attention",
    )(z3, z3, z3, zc3, zc3, tiles, mask)


def _na_bias_tiles(rpb):
    col = jnp.arange(GRID_W, dtype=jnp.int32)
    col_idx = jnp.clip(col[None, :] - col[:, None], -(NA_WIN_COLS - 1), NA_WIN_COLS - 1) + NA_WIN_COLS - 1
    t = rpb.astype(F32)[:, :, col_idx]
    t = jnp.pad(t, ((0, 0), (1, 1), (0, 0), (0, 0)))
    return jnp.concatenate([t[:, :-1], t[:, 1:]], axis=-1)


def _na_mask(rows):
    nblk = rows // NA_QROWS
    col = np.arange(GRID_W)
    col_start = np.clip(col - NA_WIN_COLS // 2, 0, GRID_W - NA_WIN_COLS)
    col_ok = (col[None, :] >= col_start[:, None]) & (col[None, :] < col_start[:, None] + NA_WIN_COLS)
    qrow = NA_QROWS * np.arange(nblk)[:, None] + np.arange(NA_QROWS)[None, :]
    krow = np.array([_na_key_row0(b, rows) for b in range(nblk)])[:, None] + np.arange(NA_KROWS)[None, :]
    win0 = np.clip(qrow - NA_WIN_ROWS // 2, 0, rows - NA_WIN_ROWS)[:, :, None]
    row_ok = (krow[:, None, :] >= win0) & (krow[:, None, :] < win0 + NA_WIN_ROWS)
    ok = row_ok[:, :, None, :, None] & col_ok[None, None, :, None, :]
    return jnp.asarray(np.where(ok, 0.0, NEG).astype(np.float32).reshape(
        nblk, NA_QROWS * GRID_W, NA_KROWS * GRID_W))


def _rms(x, g):
    return x * lax.rsqrt(jnp.mean(x * x, axis=-1, keepdims=True) + RMS_EPS) * g


def _rope(x, cos, sin_signed):
    lane = lax.broadcasted_iota(jnp.int32, x.shape, 1)
    nxt = pltpu.roll(x, HEAD_DIM - 1, 1)
    prv = pltpu.roll(x, 1, 1)
    partner = jnp.where((lane & 1) == 0, nxt, prv)
    return x * cos + partner * sin_signed


def _gqa_kernel(q0_ref, q1_ref, q2_ref, k_ref, v_ref, kc_ref, vc_ref, cosk_ref, sink_ref,
                cosq_ref, sinq_ref, qn_ref, kn_ref, o_ref, ks, vs, *, S, L):
    @pl.when(pl.program_id(2) == 0)
    def _():
        k = _rope(_rms(k_ref[0].astype(F32), kn_ref[...]), cosk_ref[...], sink_ref[...])
        ks[0:S, :] = k.astype(BF16)
        ks[S:S + L, :] = _rms(kc_ref[0].astype(F32), kn_ref[...]).astype(BF16)
        vs[0:S, :] = v_ref[0]
        vs[S:S + L, :] = vc_ref[0]

    for g, q_ref in enumerate((q0_ref, q1_ref, q2_ref)):
        q = _rope(_rms(q_ref[0].astype(F32), qn_ref[...]), cosq_ref[...], sinq_ref[...]) * ATT_SCALE
        s = _nt_dot(q.astype(BF16), ks[...])
        m = jnp.max(s, axis=-1, keepdims=True)
        p = jnp.exp(s - m)
        l = jnp.sum(p, axis=-1, keepdims=True)
        o = jnp.dot(p.astype(BF16), vs[...], preferred_element_type=F32) / l
        o_ref[0, :, g * HEAD_DIM:(g + 1) * HEAD_DIM] = o.astype(o_ref.dtype)


def _gqa_attention(z3, zc3, cos, sin, qn, kn):
    B, S, _ = z3.shape
    L = zc3.shape[1]
    tq = _tile(S, 512)
    qspec = lambda g: pl.BlockSpec(
        (1, tq, HEAD_DIM), lambda b, h, i: (b, i, OFF_GQ_Q // HEAD_DIM + h * GQ_GROUP + g))
    kv = lambda n, off: pl.BlockSpec((1, n, HEAD_DIM), lambda b, h, i: (b, 0, off // HEAD_DIM + h))
    full = pl.BlockSpec((S, HEAD_DIM), lambda b, h, i: (0, 0))
    tile = pl.BlockSpec((tq, HEAD_DIM), lambda b, h, i: (i, 0))
    vec = pl.BlockSpec((1, HEAD_DIM), lambda b, h, i: (0, 0))
    return pl.pallas_call(
        functools.partial(_gqa_kernel, S=S, L=L),
        grid=(B, GQ_KV_HEADS, S // tq),
        in_specs=[qspec(0), qspec(1), qspec(2), kv(S, OFF_GQ_K), kv(S, OFF_GQ_V),
                  kv(L, OFF_GQ_K), kv(L, OFF_GQ_V), full, full, tile, tile, vec, vec],
        out_specs=pl.BlockSpec((1, tq, GQ_GROUP * HEAD_DIM), lambda b, h, i: (b, i, h)),
        out_shape=jax.ShapeDtypeStruct((B, S, GQ_W), BF16),
        scratch_shapes=[pltpu.VMEM((S + L, HEAD_DIM), BF16), pltpu.VMEM((S + L, HEAD_DIM), BF16)],
        compiler_params=_params("parallel", "parallel", "arbitrary"),
        name="gqa_attention",
    )(z3, z3, z3, z3, z3, zc3, zc3, cos, sin, cos, sin, qn, kn)


def _rope_tables(S):
    t = jnp.arange(S, dtype=jnp.int32)
    row = (t // GRID_W).astype(F32)
    col = (t % GRID_W).astype(F32)
    axis_dim = HEAD_DIM // 2
    inv_freq = 1.0 / (ROPE_THETA ** (jnp.arange(0, axis_dim, 2, dtype=F32) / axis_dim))
    ang = jnp.concatenate([row[:, None] * inv_freq, col[:, None] * inv_freq], axis=-1)
    cos = jnp.repeat(jnp.cos(ang), 2, axis=-1)
    sin = jnp.repeat(jnp.sin(ang), 2, axis=-1)
    sign = jnp.where(jnp.arange(HEAD_DIM) % 2 == 0, -1.0, 1.0).astype(F32)
    return cos, sin * sign


def _ctx_attn_kernel(q_ref, k_ref, v_ref, qn_ref, kn_ref, o_ref, *, norm):
    q = q_ref[0]
    k = k_ref[0]
    if norm:
        q = _rms(q.astype(F32), qn_ref[...]).astype(BF16)
        k = _rms(k.astype(F32), kn_ref[...]).astype(BF16)
    s = _nt_dot(q, k) * ATT_SCALE
    m = jnp.max(s, axis=-1, keepdims=True)
    p = jnp.exp(s - m)
    l = jnp.sum(p, axis=-1, keepdims=True)
    o_ref[0] = (jnp.dot(p.astype(BF16), v_ref[0], preferred_element_type=F32) / l).astype(o_ref.dtype)


def _ctx_attention(zc3, off_q, off_k, off_v, group, norm, qn, kn):
    B, L, _ = zc3.shape
    n_heads = NA_HEADS
    spec = lambda off, div: pl.BlockSpec((1, L, HEAD_DIM), lambda b, h: (b, 0, off // HEAD_DIM + h // div))
    vec = pl.BlockSpec((1, HEAD_DIM), lambda b, h: (0, 0))
    return pl.pallas_call(
        functools.partial(_ctx_attn_kernel, norm=norm),
        grid=(B, n_heads),
        in_specs=[spec(off_q, 1), spec(off_k, group), spec(off_v, group), vec, vec],
        out_specs=pl.BlockSpec((1, L, HEAD_DIM), lambda b, h: (b, 0, h)),
        out_shape=jax.ShapeDtypeStruct((B, L, n_heads * HEAD_DIM), BF16),
        compiler_params=_params("parallel", "parallel"),
        name="ctx_attention",
    )(zc3, zc3, zc3, qn, kn)


SSM_SLABS = SSM_WIDTH // LANES
SLAB_STATE = SSM_LANES // SSM_SLABS


def _s5_kernel(uf_ref, ub_ref, wb_ref, wc_ref, a_ref, h0_ref, yf_ref, yb_ref, hout_ref,
               br, bi, hs, yt, *, tc, n, nb):
    j = pl.program_id(0)
    rows = tc * nb

    @pl.when(j == 0)
    def _():
        hs[...] = h0_ref[...]

    for d, (u_ref, y_ref) in enumerate(((uf_ref, yf_ref), (ub_ref, yb_ref))):
        u = u_ref[...].reshape(rows, SSM_WIDTH).astype(BF16)
        for k in range(SSM_SLABS):
            r = jnp.dot(u[:, k * LANES:(k + 1) * LANES], wb_ref[d, k], preferred_element_type=F32)
            br[:, k * SLAB_STATE:(k + 1) * SLAB_STATE] = r[:, :SLAB_STATE]
            bi[:, k * SLAB_STATE:(k + 1) * SLAB_STATE] = r[:, SLAB_STATE:]
        ar = jnp.broadcast_to(a_ref[d, 0], (nb, SSM_LANES))
        ai = jnp.broadcast_to(a_ref[d, 1], (nb, SSM_LANES))

        def step(i, carry, d=d, ar=ar, ai=ai):
            hr, hi = carry
            t = i if d == 0 else tc - 1 - i
            row = pl.multiple_of(t * nb, nb)
            nr = ar * hr - ai * hi + br[pl.ds(row, nb), :]
            ni = ar * hi + ai * hr + bi[pl.ds(row, nb), :]
            br[pl.ds(row, nb), :] = nr
            bi[pl.ds(row, nb), :] = ni
            return nr, ni

        hr, hi = lax.fori_loop(0, tc, step, (hs[2 * d], hs[2 * d + 1]))
        hs[2 * d] = hr
        hs[2 * d + 1] = hi
        for k in range(SSM_SLABS):
            sl = slice(k * SLAB_STATE, (k + 1) * SLAB_STATE)
            y = (jnp.dot(br[:, sl].astype(BF16), wc_ref[d, k, 0:SLAB_STATE, :], preferred_element_type=F32)
                 + jnp.dot(bi[:, sl].astype(BF16), wc_ref[d, k, SLAB_STATE:, :], preferred_element_type=F32))
            yt[k] = y
        for b in range(nb):
            for k in range(SSM_SLABS):
                y_ref[b, :, k * LANES:(k + 1) * LANES] = yt[k, pl.ds(b, tc, stride=nb), :]

    @pl.when(j == n - 1)
    def _():
        hout_ref[...] = hs[...]


def _s5_scan(u_t, wb, wc, a, h0):
    T, nb, _ = u_t.shape
    tc = _tile(T, 64)
    n = T // tc
    ublk = lambda rev: pl.BlockSpec((tc, nb, SSM_WIDTH), (lambda j: (n - 1 - j, 0, 0)) if rev else (lambda j: (j, 0, 0)))
    yblk = lambda rev: pl.BlockSpec((nb, tc, SSM_WIDTH), (lambda j: (0, n - 1 - j, 0)) if rev else (lambda j: (0, j, 0)))
    const = lambda shape: pl.BlockSpec(shape, lambda j: (0,) * len(shape))
    return pl.pallas_call(
        functools.partial(_s5_kernel, tc=tc, n=n, nb=nb),
        grid=(n,),
        in_specs=[ublk(False), ublk(True), const(wb.shape), const(wc.shape), const(a.shape), const(h0.shape)],
        out_specs=[yblk(False), yblk(True), const(h0.shape)],
        out_shape=[jax.ShapeDtypeStruct((nb, T, SSM_WIDTH), F32), jax.ShapeDtypeStruct((nb, T, SSM_WIDTH), F32),
                   jax.ShapeDtypeStruct(h0.shape, F32)],
        scratch_shapes=[pltpu.VMEM((tc * nb, SSM_LANES), F32), pltpu.VMEM((tc * nb, SSM_LANES), F32),
                        pltpu.VMEM(h0.shape, F32), pltpu.VMEM((SSM_SLABS, tc * nb, LANES), F32)],
        compiler_params=_params("arbitrary"),
        name="s5_scan",
    )(u_t, u_t, wb, wc, a, h0)


def _s5_weights(lam_r, lam_i, log_dt, b_r, b_i, c_r, c_i):
    lam_r, lam_i, log_dt = lam_r.astype(F32), lam_i.astype(F32), log_dt.astype(F32)
    dt = jnp.exp(log_dt)[..., None]
    mag = jnp.exp(lam_r * dt)
    ar, ai = mag * jnp.cos(lam_i * dt), mag * jnp.sin(lam_i * dt)
    den = lam_r * lam_r + lam_i * lam_i
    nr, ni = ar - 1.0, ai
    kr, ki = (nr * lam_r + ni * lam_i) / den, (ni * lam_r - nr * lam_i) / den
    b_r, b_i = b_r.astype(F32), b_i.astype(F32)
    bbr = kr[..., None] * b_r - ki[..., None] * b_i
    bbi = kr[..., None] * b_i + ki[..., None] * b_r
    gs = SSM_GROUPS // SSM_SLABS
    eye = jnp.eye(gs, dtype=F32)

    def in_slab(w):
        w = w.reshape(2, SSM_SLABS, gs, SSM_STATE, SSM_GROUP)
        return jnp.einsum('dkgph,gj->dkghjp', w, eye).reshape(2, SSM_SLABS, gs * SSM_GROUP, gs * SSM_STATE)

    def out_slab(w):
        w = w.reshape(2, SSM_SLABS, gs, SSM_GROUP, SSM_STATE)
        return jnp.einsum('dkghp,gj->dkjpgh', w, eye).reshape(2, SSM_SLABS, gs * SSM_STATE, gs * SSM_GROUP)

    wb = jnp.concatenate([in_slab(bbr), in_slab(bbi)], axis=-1).astype(BF16)
    wc = jnp.concatenate([out_slab(c_r.astype(F32)), -out_slab(c_i.astype(F32))], axis=-2).astype(BF16)
    a = jnp.stack([ar.reshape(2, 1, SSM_LANES), ai.reshape(2, 1, SSM_LANES)], axis=1)
    return wb, wc, a


def _gelu_tanh(x):
    return 0.5 * x * (1.0 + jnp.tanh(math.sqrt(2.0 / math.pi) * (x + 0.044715 * (x * x * x))))


def _glu_kernel(u_ref, yf_ref, yb_ref, d_ref, w_ref, b_ref, o_ref):
    y = u_ref[0].astype(F32) * d_ref[...] + yf_ref[...] + yb_ref[...]
    t = _gelu_tanh(y).astype(BF16)
    r = jnp.dot(t, w_ref[...], preferred_element_type=F32) + b_ref[...]
    o_ref[0] = (r[:, :SSM_WIDTH] * _sigmoid(r[:, SSM_WIDTH:])).astype(o_ref.dtype)


def _glu(z3, yf, yb, d, w, b):
    B, S, _ = z3.shape
    ts = _tile(S, 512)
    return pl.pallas_call(
        _glu_kernel,
        grid=(B, S // ts),
        in_specs=[pl.BlockSpec((1, ts, SSM_WIDTH), lambda b, i: (b, i, OFF_SU // SSM_WIDTH)),
                  pl.BlockSpec((None, ts, SSM_WIDTH), lambda b, i: (b, i, 0)),
                  pl.BlockSpec((None, ts, SSM_WIDTH), lambda b, i: (b, i, 0)),
                  pl.BlockSpec((1, SSM_WIDTH), lambda b, i: (0, 0)),
                  pl.BlockSpec((SSM_WIDTH, 2 * SSM_WIDTH), lambda b, i: (0, 0)),
                  pl.BlockSpec((1, 2 * SSM_WIDTH), lambda b, i: (0, 0))],
        out_specs=pl.BlockSpec((1, ts, SSM_WIDTH), lambda b, i: (b, i, 0)),
        out_shape=jax.ShapeDtypeStruct((B, S, SSM_WIDTH), BF16),
        compiler_params=_params("parallel", "parallel"),
        name="s5_glu",
    )(z3, yf, yb, d, w, b)


def _merge_kernel(a_ref, b_ref, s_ref, ga_ref, gb_ref, gs_ref, wa_ref, wb_ref, ws_ref, o_ref):
    m = (_sigmoid(ga_ref[...].astype(F32)) * jnp.dot(a_ref[...], wa_ref[...], preferred_element_type=F32)
         + _sigmoid(gb_ref[...].astype(F32)) * jnp.dot(b_ref[...], wb_ref[...], preferred_element_type=F32)
         + _sigmoid(gs_ref[...].astype(F32)) * jnp.dot(s_ref[...], ws_ref[...], preferred_element_type=F32))
    o_ref[...] = m.astype(o_ref.dtype)


def _merge(a, b, s, z2, wa, wb, ws):
    M = a.shape[0]
    D = wa.shape[1]
    tm, tn = _tile(M, 1024), _tile(D, 1024)
    row = lambda w: pl.BlockSpec((tm, w), lambda i, j: (i, 0))
    gate = lambda br: pl.BlockSpec((tm, tn), lambda i, j: (i, (OFF_GATE + br * D) // tn + j))
    wsp = lambda w: pl.BlockSpec((w, tn), lambda i, j: (0, j))
    return pl.pallas_call(
        _merge_kernel,
        grid=(M // tm, D // tn),
        in_specs=[row(NA_W), row(GQ_W), row(SSM_WIDTH), gate(0), gate(1), gate(2),
                  wsp(NA_W), wsp(GQ_W), wsp(SSM_WIDTH)],
        out_specs=pl.BlockSpec((tm, tn), lambda i, j: (i, j)),
        out_shape=jax.ShapeDtypeStruct((M, D), BF16),
        compiler_params=_params("parallel", "arbitrary"),
        name="merge",
    )(a, b, s, z2, z2, z2, wa, wb, ws)


def _layer_norm(x, g, b):
    mu = jnp.mean(x, axis=-1, keepdims=True)
    xc = x - mu
    var = jnp.mean(xc * xc, axis=-1, keepdims=True)
    return xc * lax.rsqrt(var + LN_EPS) * g + b


def _outproj_kernel(m_ref, w_ref, h_ref, g1_ref, sh_ref, sc_ref, lng_ref, lnb_ref, rwh_ref, rwl_ref, rb_ref,
                    cnt0_ref, hn_ref, tokr_ref, tokb_ref, idx_ref, wts_ref, rank_ref, cnt_ref, cnt, *, tm, D):
    @pl.when(pl.program_id(0) == 0)
    def _():
        cnt[...] = cnt0_ref[...]

    mix = jnp.dot(m_ref[...], w_ref[...], preferred_element_type=F32)
    hn = _layer_norm(DN_ALPHA * h_ref[...] + g1_ref[0] * mix, lng_ref[...], lnb_ref[...])
    hn_ref[...] = hn
    tok = hn * (1.0 + sc_ref[0]) + sh_ref[0]
    tokr_ref[...] = tok
    t_hi = tok.astype(BF16)
    tokb_ref[...] = t_hi
    t_lo = (tok - t_hi.astype(F32)).astype(BF16)
    logits = (jnp.dot(t_hi, rwh_ref[...], preferred_element_type=F32)
              + jnp.dot(t_lo, rwh_ref[...], preferred_element_type=F32)
              + jnp.dot(t_hi, rwl_ref[...], preferred_element_type=F32))
    scores = _sigmoid(logits)
    sel = scores + rb_ref[...]
    lane = lax.broadcasted_iota(jnp.int32, sel.shape, 1).astype(F32)
    slot = lax.broadcasted_iota(jnp.int32, (tm, LANES), 1)
    idx_acc = jnp.zeros((tm, LANES), F32)
    w_acc = jnp.zeros((tm, LANES), F32)
    hits = []
    for k in range(TOP_K):
        mx = jnp.max(sel, axis=-1, keepdims=True)
        am = jnp.min(jnp.where(sel == mx, lane, float(N_EXPERTS)), axis=-1, keepdims=True)
        hit = lane == am
        hits.append(hit)
        wk = jnp.sum(jnp.where(hit, scores, 0.0), axis=-1, keepdims=True)
        idx_acc = jnp.where(slot == k, am, idx_acc)
        w_acc = jnp.where(slot == k, wk, w_acc)
        sel = jnp.where(hit, -jnp.inf, sel)
    wsum = jnp.sum(w_acc, axis=-1, keepdims=True)
    idx_ref[...] = idx_acc.astype(jnp.int32)
    wts_ref[...] = w_acc / wsum * ROUTED_SCALE

    picked = jnp.zeros(sel.shape, F32)
    for hit in hits:
        picked = jnp.where(hit, 1.0, picked)
    ri = lax.broadcasted_iota(jnp.int32, (tm, tm), 0)
    ci = lax.broadcasted_iota(jnp.int32, (tm, tm), 1)
    below = jnp.where(ci < ri, 1.0, 0.0).astype(BF16)
    before = jnp.dot(below, picked.astype(BF16), preferred_element_type=F32) + cnt[...]
    rank_acc = jnp.zeros((tm, LANES), F32)
    for k, hit in enumerate(hits):
        rk = jnp.sum(jnp.where(hit, before, 0.0), axis=-1, keepdims=True)
        rank_acc = jnp.where(slot == k, rk, rank_acc)
    rank_ref[...] = rank_acc.astype(jnp.int32)
    cnt[...] = cnt[...] + jnp.sum(picked, axis=0, keepdims=True)
    cnt_ref[...] = cnt[...]


def _outproj_ln_route(m, w_out, h2, mod3, row_fn, ln_g, ln_b, rw_hi, rw_lo, rbias, cnt0, tm=256):
    M, D = m.shape
    tm = _tile(M, tm)
    vec = lambda k: pl.BlockSpec((1, 1, D), lambda i: (row_fn(i), 0, k))
    const = lambda shape: pl.BlockSpec(shape, lambda i: (0,) * len(shape))
    rowblk = lambda w: pl.BlockSpec((tm, w), lambda i: (i, 0))
    return pl.pallas_call(
        functools.partial(_outproj_kernel, tm=tm, D=D),
        grid=(M // tm,),
        in_specs=[rowblk(D), const((D, D)), rowblk(D), vec(2), vec(3), vec(4), const((1, D)), const((1, D)),
                  const((D, N_EXPERTS)), const((D, N_EXPERTS)), const((1, N_EXPERTS)), const((1, N_EXPERTS))],
        out_specs=[rowblk(D), rowblk(D), rowblk(D),
                   rowblk(LANES), rowblk(LANES), rowblk(LANES), const((1, N_EXPERTS))],
        out_shape=[jax.ShapeDtypeStruct((M, D), F32), jax.ShapeDtypeStruct((M, D), F32),
                   jax.ShapeDtypeStruct((M, D), BF16), jax.ShapeDtypeStruct((M, LANES), jnp.int32),
                   jax.ShapeDtypeStruct((M, LANES), F32), jax.ShapeDtypeStruct((M, LANES), jnp.int32),
                   jax.ShapeDtypeStruct((1, N_EXPERTS), F32)],
        scratch_shapes=[pltpu.VMEM((1, N_EXPERTS), F32)],
        compiler_params=_params("arbitrary"),
        name="outproj_ln_route",
    )(m, w_out, h2, mod3, mod3, mod3, ln_g, ln_b, rw_hi, rw_lo, rbias, cnt0)


def _swiglu_kernel(x_ref, w13_ref, w2_ref, o_ref, *, E):
    h = jnp.dot(x_ref[...], w13_ref[...], preferred_element_type=F32)
    g = h[:, :E]
    act = (g * _sigmoid(g) * h[:, E:]).astype(BF16)
    o_ref[...] = jnp.dot(act, w2_ref[...], preferred_element_type=F32).astype(o_ref.dtype)


def _shared_expert(tokb, w13, w2):
    T, D = tokb.shape
    E = w2.shape[0]
    tm = _tile(T, 512)
    return pl.pallas_call(
        functools.partial(_swiglu_kernel, E=E),
        grid=(T // tm,),
        in_specs=[pl.BlockSpec((tm, D), lambda i: (i, 0)), pl.BlockSpec((D, 2 * E), lambda i: (0, 0)),
                  pl.BlockSpec((E, D), lambda i: (0, 0))],
        out_specs=pl.BlockSpec((tm, D), lambda i: (i, 0)),
        out_shape=jax.ShapeDtypeStruct((T, D), BF16),
        compiler_params=_params("parallel"),
        name="shared_expert",
    )(tokb, w13, w2)


def _route_plan(idx, rank, counts, tm):
    T = idx.shape[0]
    counts = counts.reshape(N_EXPERTS).astype(jnp.int32)
    padded = (counts + tm - 1) // tm * tm
    pend = jnp.cumsum(padded)
    pstart = pend - padded
    experts = jnp.arange(N_EXPERTS, dtype=jnp.int32)
    dest = (jnp.sum(jnp.where(idx[..., None] == experts, pstart, 0), axis=-1) + rank).astype(jnp.int32)
    n_blk = -(-(T * TOP_K + N_EXPERTS * (tm - 1)) // tm)
    first = jnp.arange(n_blk, dtype=jnp.int32) * tm
    blk_exp = jnp.minimum(jnp.sum(first[:, None] >= pend[None, :], axis=1), N_EXPERTS - 1).astype(jnp.int32)
    n_used = (pend[-1:] // tm).astype(jnp.int32)
    return dest, blk_exp, n_used, (pstart + counts).astype(jnp.int32), (padded - counts).astype(jnp.int32)


def _dispatch_kernel(ps_ref, pc_ref, nu_ref, dest_ref, tok_ref, xs_hbm, zbuf, sem, *, tb, n, tm, n_blk):
    i = pl.program_id(0)

    def pad_rows(do):
        def per_expert(e, c):
            def per_row(r, c2):
                do(pltpu.make_async_copy(zbuf.at[pl.ds(0, 1)], xs_hbm.at[pl.ds(ps_ref[e] + r, 1)], sem.at[1]))
                return c2
            return lax.fori_loop(0, pc_ref[e], per_row, c)
        lax.fori_loop(0, N_EXPERTS, per_expert, 0)

        def per_block(b, c):
            do(pltpu.make_async_copy(zbuf, xs_hbm.at[pl.ds(pl.multiple_of(b * tm, tm), tm)], sem.at[1]))
            return c
        lax.fori_loop(nu_ref[0], n_blk, per_block, 0)

    @pl.when(i == 0)
    def _():
        zbuf[...] = jnp.zeros(zbuf.shape, zbuf.dtype)
        pad_rows(lambda cp: cp.start())

    def body(t, c):
        src = tok_ref.at[pl.ds(t, 1)]
        for k in range(TOP_K):
            d = dest_ref[0, 0, t * TOP_K + k]
            pltpu.make_async_copy(src, xs_hbm.at[pl.ds(d, 1)], sem.at[0]).start()
        return c
    lax.fori_loop(0, tb, body, 0)
    for k in range(TOP_K):
        pltpu.make_async_copy(tok_ref, xs_hbm.at[pl.ds(0, tb)], sem.at[0]).wait()

    @pl.when(i == n - 1)
    def _():
        pad_rows(lambda cp: cp.wait())


def _dispatch(tokr, dest, pad_start, pad_cnt, n_used, n_blk, tm, tb=256):
    T, D = tokr.shape
    tb = _tile(T, tb)
    n = T // tb
    grid_spec = pltpu.PrefetchScalarGridSpec(
        num_scalar_prefetch=3,
        grid=(n,),
        in_specs=[pl.BlockSpec((1, 1, tb * TOP_K), lambda i, ps, pc, nu: (i, 0, 0), memory_space=pltpu.SMEM),
                  pl.BlockSpec((tb, D), lambda i, ps, pc, nu: (i, 0))],
        out_specs=pl.BlockSpec(memory_space=pl.ANY),
        scratch_shapes=[pltpu.VMEM((tm, D), F32), pltpu.SemaphoreType.DMA((2,))],
    )
    return pl.pallas_call(
        functools.partial(_dispatch_kernel, tb=tb, n=n, tm=tm, n_blk=n_blk),
        grid_spec=grid_spec,
        out_shape=jax.ShapeDtypeStruct((n_blk * tm, D), F32),
        compiler_params=pltpu.CompilerParams(dimension_semantics=("arbitrary",), vmem_limit_bytes=VMEM_LIMIT,
                                             disable_bounds_checks=True),
        name="moe_dispatch",
    )(pad_start, pad_cnt, n_used, dest.reshape(n, 1, tb * TOP_K), tokr)


def _expert_kernel(be_ref, nu_ref, x_ref, w1_ref, w3_ref, w2_ref, y_ref, w13s, w2s, *, E):
    i = pl.program_id(0)

    @pl.when((i == 0) | (be_ref[i] != be_ref[jnp.maximum(i - 1, 0)]))
    def _():
        w13s[:, 0:E] = w1_ref[...].astype(BF16)
        w13s[:, E:2 * E] = w3_ref[...].astype(BF16)
        w2s[...] = w2_ref[...].astype(BF16)

    @pl.when(i < nu_ref[0])
    def _():
        h = jnp.dot(x_ref[...].astype(BF16), w13s[...], preferred_element_type=F32)
        g = h[:, :E]
        act = (g * _sigmoid(g) * h[:, E:]).astype(BF16)
        y_ref[...] = jnp.dot(act, w2s[...], preferred_element_type=F32)

    @pl.when(i >= nu_ref[0])
    def _():
        y_ref[...] = jnp.zeros(y_ref.shape, y_ref.dtype)


def _routed_experts(xs, blk_exp, n_used, w1, w3, w2, l, tm):
    n = blk_exp.shape[0]
    _, _, D, E = w1.shape
    wspec = lambda a, b: pl.BlockSpec((None, None, a, b), lambda i, be, nu: (l, be[i], 0, 0))
    grid_spec = pltpu.PrefetchScalarGridSpec(
        num_scalar_prefetch=2,
        grid=(n,),
        in_specs=[pl.BlockSpec((tm, D), lambda i, be, nu: (jnp.where(i < nu[0], i, 0), 0)),
                  wspec(D, E), wspec(D, E), wspec(E, D)],
        out_specs=pl.BlockSpec((tm, D), lambda i, be, nu: (i, 0)),
        scratch_shapes=[pltpu.VMEM((D, 2 * E), BF16), pltpu.VMEM((E, D), BF16)],
    )
    return pl.pallas_call(
        functools.partial(_expert_kernel, E=E),
        grid_spec=grid_spec,
        out_shape=jax.ShapeDtypeStruct((n * tm, D), F32),
        compiler_params=_params("arbitrary"),
        name="routed_experts",
    )(blk_exp, n_used, xs, w1, w3, w2)


def _combine_kernel(dcur_ref, dnxt_ref, h_ref, sh_ref, w_ref, g2_ref, lng_ref, lnb_ref, ys_hbm, o_ref,
                    ybuf, sem, *, tb, D, n):
    i = pl.program_id(0)
    rows = tb * TOP_K

    def gather(slot, d_ref):
        def body(g, c):
            t0 = pl.multiple_of(g * SUBLANES, SUBLANES)
            dst0 = pl.multiple_of(slot * rows + t0, SUBLANES)
            for u in range(SUBLANES):
                for k in range(TOP_K):
                    d = d_ref[0, 0, (t0 + u) * TOP_K + k]
                    pltpu.make_async_copy(ys_hbm.at[pl.ds(d, 1)], ybuf.at[pl.ds(dst0 + (k * tb + u), 1)],
                                          sem.at[slot]).start()
            return c
        lax.fori_loop(0, tb // SUBLANES, body, 0)

    @pl.when(i == 0)
    def _():
        gather(0, dcur_ref)

    @pl.when(i + 1 < n)
    def _():
        gather((i + 1) % 2, dnxt_ref)

    slot = i % 2
    base = slot * rows
    pltpu.make_async_copy(ys_hbm.at[pl.ds(0, rows)], ybuf.at[pl.ds(pl.multiple_of(base, rows), rows)],
                          sem.at[slot]).wait()
    w = w_ref[...]
    f = sh_ref[...].astype(F32)
    for k in range(TOP_K):
        f = f + w[:, k:k + 1] * ybuf[pl.ds(pl.multiple_of(base + k * tb, tb), tb), :]
    o_ref[...] = _layer_norm(DN_ALPHA * h_ref[...] + g2_ref[0] * f, lng_ref[...], lnb_ref[...])


def _combine_ln(h2, shared, wts, ys, dest, tok_off, mod3, row_fn, ln_g, ln_b, tb=128):
    M, D = h2.shape
    tb = _tile(M, tb)
    n = M // tb
    off = tok_off // tb
    dest3 = dest.reshape(-1, 1, tb * TOP_K)
    rowblk = lambda w, o: pl.BlockSpec((tb, w), lambda i: (i + o, 0))
    const = lambda shape: pl.BlockSpec(shape, lambda i: (0,) * len(shape))
    smem = lambda fn: pl.BlockSpec((1, 1, tb * TOP_K), fn, memory_space=pltpu.SMEM)
    return pl.pallas_call(
        functools.partial(_combine_kernel, tb=tb, D=D, n=n),
        grid=(n,),
        in_specs=[smem(lambda i: (i + off, 0, 0)), smem(lambda i: (jnp.minimum(i + 1, n - 1) + off, 0, 0)),
                  rowblk(D, 0), rowblk(D, off), rowblk(LANES, off),
                  pl.BlockSpec((1, 1, D), lambda i: (row_fn(i), 0, 5)), const((1, D)), const((1, D)),
                  pl.BlockSpec(memory_space=pl.ANY)],
        out_specs=rowblk(D, 0),
        out_shape=jax.ShapeDtypeStruct((M, D), F32),
        scratch_shapes=[pltpu.VMEM((2 * tb * TOP_K, D), F32), pltpu.SemaphoreType.DMA((2,))],
        compiler_params=pltpu.CompilerParams(dimension_semantics=("arbitrary",), vmem_limit_bytes=VMEM_LIMIT,
                                             disable_bounds_checks=True),
        name="combine_ln",
    )(dest3, dest3, h2, shared, wts, mod3, ln_g, ln_b, ys)


MOE_TM = 256


def kernel(x, c, ctx, c_ctx, w_mod, b_mod, w_in, na_rpb, gq_q_norm, gq_k_norm, ssm_lam_re, ssm_lam_im, ssm_log_dt, ssm_b_re, ssm_b_im, ssm_c_re, ssm_c_im, ssm_d, ssm_glu_w, ssm_glu_b, w_br_na, w_br_gq, w_br_ssm, w_out, ln1_g, ln1_b, router_w, router_bias, exp_w1, exp_w3, exp_w2, sh_w1, sh_w3, sh_w2, ln2_g, ln2_b):
    B, S, D = x.shape
    L = ctx.shape[1]
    assert B < 2 * SUBLANES and D % LANES == 0
    cos, sin = _rope_tables(S)
    na_mask = _na_mask(S // GRID_W)
    c_all = jnp.zeros((2 * SUBLANES, D), F32).at[:B].set(c).at[B].set(c_ctx)
    lat_row = lambda b: b
    ctx_row = lambda b: B
    h, hc = x, ctx
    for l in range(DEPTH):
        need_ctx = l < DEPTH - 1
        mod3 = _modvec(c_all, w_mod, l, b_mod[l])[:, None, :]
        vec = lambda p: p[l].reshape(1, -1).astype(F32)

        w_in_b = w_in[l].astype(BF16)
        u = _modulate(h, mod3, lat_row, 0, 1)
        uc = _modulate(hc, mod3, ctx_row, 0, 1)
        z2 = _matmul(u.reshape(B * S, D), w_in_b, BF16)
        zc2 = _matmul(uc.reshape(B * L, D), w_in_b if need_ctx else w_in_b[:, :CTX_IN_W], BF16)
        z3, zc3 = z2.reshape(B, S, -1), zc2.reshape(B, L, -1)

        a_out = _na_attention(z3, zc3, _na_bias_tiles(na_rpb[l]), na_mask)
        qn, kn = vec(gq_q_norm), vec(gq_k_norm)
        b_out = _gqa_attention(z3, zc3, cos, sin, qn, kn)

        wb, wc, a = _s5_weights(ssm_lam_re[l], ssm_lam_im[l], ssm_log_dt[l], ssm_b_re[l], ssm_b_im[l],
                                ssm_c_re[l], ssm_c_im[l])
        su = lambda t: t[:, :, OFF_SU:OFF_SU + SSM_WIDTH].astype(F32).transpose(1, 0, 2)
        h0 = jnp.zeros((4, B, SSM_LANES), F32)
        ycf, ycb, hfin = _s5_scan(su(zc3), wb, wc, a, h0)
        yf, yb, _ = _s5_scan(su(z3), wb, wc, a, hfin)
        glu_w, glu_b, dvec = ssm_glu_w[l].astype(BF16), vec(ssm_glu_b), vec(ssm_d)
        c_out = _glu(z3, yf, yb, dvec, glu_w, glu_b)

        wa, wg, ws, wo = (w[l].astype(BF16) for w in (w_br_na, w_br_gq, w_br_ssm, w_out))
        m = _merge(a_out.reshape(B * S, -1), b_out.reshape(B * S, -1), c_out.reshape(B * S, -1), z2, wa, wg, ws)
        rw = router_w[l].astype(F32)
        rw_hi = rw.astype(BF16)
        rw_lo = (rw - rw_hi.astype(F32)).astype(BF16)
        rbias = vec(router_bias)
        g1, b1 = vec(ln1_g), vec(ln1_b)
        tm_o = _tile(S, 256)
        h2, tokr, tokb, idx, wts, rank, counts = _outproj_ln_route(
            m, wo, h.reshape(B * S, D), mod3, lambda i: i // (S // tm_o), g1, b1, rw_hi, rw_lo, rbias,
            jnp.zeros((1, N_EXPERTS), F32), tm=tm_o)
        if need_ctx:
            a_ctx = _ctx_attention(zc3, OFF_NA_Q, OFF_NA_K, OFF_NA_V, 1, False, qn, kn)
            b_ctx = _ctx_attention(zc3, OFF_GQ_Q, OFF_GQ_K, OFF_GQ_V, GQ_GROUP, True, qn, kn)
            c_ctx_out = _glu(zc3, ycf, ycb, dvec, glu_w, glu_b)
            mc = _merge(a_ctx.reshape(B * L, -1), b_ctx.reshape(B * L, -1), c_ctx_out.reshape(B * L, -1),
                        zc2, wa, wg, ws)
            tm_c = _tile(L, 256)
            hc2, tokr_c, tokb_c, idx_c, wts_c, rank_c, counts = _outproj_ln_route(
                mc, wo, hc.reshape(B * L, D), mod3, lambda i: B, g1, b1, rw_hi, rw_lo, rbias, counts, tm=tm_c)
            tokr = jnp.concatenate([tokr, tokr_c], axis=0)
            tokb = jnp.concatenate([tokb, tokb_c], axis=0)
            idx = jnp.concatenate([idx, idx_c], axis=0)
            wts = jnp.concatenate([wts, wts_c], axis=0)
            rank = jnp.concatenate([rank, rank_c], axis=0)

        w13 = jnp.concatenate([sh_w1[l], sh_w3[l]], axis=1).astype(BF16)
        shared = _shared_expert(tokb, w13, sh_w2[l].astype(BF16))
        dest, blk_exp, n_used, pad_start, pad_cnt = _route_plan(idx[:, :TOP_K], rank[:, :TOP_K], counts, MOE_TM)
        xs = _dispatch(tokr, dest, pad_start, pad_cnt, n_used, blk_exp.shape[0], MOE_TM)
        ys = _routed_experts(xs, blk_exp, n_used, exp_w1, exp_w3, exp_w2, l, MOE_TM)
        g2, b2 = vec(ln2_g), vec(ln2_b)
        tb_l = _tile(S, 128)
        h = _combine_ln(h2, shared, wts, ys, dest, 0, mod3, lambda i: i // (S // tb_l), g2, b2,
                        tb=tb_l).reshape(B, S, D)
        if need_ctx:
            hc = _combine_ln(hc2, shared, wts, ys, dest, B * S, mod3, lambda i: B, g2, b2,
                             tb=_tile(L, 128)).reshape(B, L, D)
    return h
```

```python
import functools
import math

import jax
import jax.numpy as jnp
import numpy as np
from jax import lax
from jax.experimental import pallas as pl
from jax.experimental.pallas import tpu as pltpu

F32 = jnp.float32
BF16 = jnp.bfloat16

DEPTH = 2
GRID_W = 64
HEAD_DIM = 128
NA_HEADS = 6
NA_WIN_ROWS = 8
NA_WIN_COLS = 16
GQ_HEADS = 6
GQ_KV_HEADS = 2
GQ_GROUP = GQ_HEADS // GQ_KV_HEADS
ROPE_THETA = 10000.0
SSM_GROUP = 16
SSM_WIDTH = 512
SSM_GROUPS = SSM_WIDTH // SSM_GROUP
SSM_STATE = 64
SSM_LANES = SSM_GROUPS * SSM_STATE
N_EXPERTS = 64
TOP_K = 8
EXPERT_DIM = 512
ROUTED_SCALE = 2.5
DN_ALPHA = (2 * DEPTH) ** 0.25
LN_EPS = 1e-6
RMS_EPS = 1e-6
ATT_SCALE = HEAD_DIM ** -0.5
NEG = -1e30
NA_QROWS = 8
NA_KROWS = 16

LANES = 128
SUBLANES = 8
VMEM_LIMIT = 56 * 1024 * 1024

NA_W = NA_HEADS * HEAD_DIM
GQ_W = GQ_HEADS * HEAD_DIM
GQ_KV_W = GQ_KV_HEADS * HEAD_DIM
OFF_NA_K = 0
OFF_NA_V = OFF_NA_K + NA_W
OFF_GQ_K = OFF_NA_V + NA_W
OFF_GQ_V = OFF_GQ_K + GQ_KV_W
OFF_SU = OFF_GQ_V + GQ_KV_W
CTX_IN_W = OFF_SU + SSM_WIDTH
OFF_NA_Q = CTX_IN_W
OFF_GQ_Q = OFF_NA_Q + NA_W
OFF_GATE = OFF_GQ_Q + GQ_W


def _tile(n, pref):
    t = min(n, pref)
    while n % t:
        t //= 2
    return t


def _params(*sem):
    return pltpu.CompilerParams(dimension_semantics=sem, vmem_limit_bytes=VMEM_LIMIT)


def _sigmoid(x):
    return 1.0 / (1.0 + jnp.exp(-x))


def _modvec_kernel(c_ref, w_ref, b_ref, o_ref):
    c = c_ref[...]
    a = (c * _sigmoid(c)).astype(BF16)
    o_ref[...] = jnp.dot(a, w_ref[...].astype(BF16), preferred_element_type=F32) + b_ref[...]


def _modvec(c_all, w, l, b):
    R, D = c_all.shape
    N = w.shape[2]
    tn = _tile(N, 1024)
    return pl.pallas_call(
        _modvec_kernel,
        grid=(N // tn,),
        in_specs=[pl.BlockSpec((R, D), lambda j: (0, 0)),
                  pl.BlockSpec((None, D, tn), lambda j: (l, 0, j)),
                  pl.BlockSpec((1, tn), lambda j: (0, j))],
        out_specs=pl.BlockSpec((R, tn), lambda j: (0, j)),
        out_shape=jax.ShapeDtypeStruct((R, N), F32),
        compiler_params=_params("arbitrary"),
        name="modvec",
    )(c_all, w, b.reshape(1, N))


def _modulate_kernel(h_ref, sh_ref, sc_ref, o_ref):
    o_ref[0] = (h_ref[0] * (1.0 + sc_ref[0]) + sh_ref[0]).astype(o_ref.dtype)


def _modulate(h, mod3, row_fn, k_shift, k_scale):
    B, S, D = h.shape
    ts = _tile(S, 512)
    return pl.pallas_call(
        _modulate_kernel,
        grid=(B, S // ts),
        in_specs=[pl.BlockSpec((1, ts, D), lambda b, i: (b, i, 0)),
                  pl.BlockSpec((1, 1, D), lambda b, i: (row_fn(b), 0, k_shift)),
                  pl.BlockSpec((1, 1, D), lambda b, i: (row_fn(b), 0, k_scale))],
        out_specs=pl.BlockSpec((1, ts, D), lambda b, i: (b, i, 0)),
        out_shape=jax.ShapeDtypeStruct((B, S, D), BF16),
        compiler_params=_params("parallel", "parallel"),
        name="modulate",
    )(h, mod3, mod3)


def _mm_kernel(x_ref, w_ref, o_ref):
    o_ref[...] = jnp.dot(x_ref[...], w_ref[...], preferred_element_type=F32).astype(o_ref.dtype)


def _matmul(x, w, out_dtype, tm=1024, tn=1024):
    M, K = x.shape
    N = w.shape[1]
    tm, tn = _tile(M, tm), _tile(N, tn)
    return pl.pallas_call(
        _mm_kernel,
        grid=(M // tm, N // tn),
        in_specs=[pl.BlockSpec((tm, K), lambda i, j: (i, 0)),
                  pl.BlockSpec((K, tn), lambda i, j: (0, j))],
        out_specs=pl.BlockSpec((tm, tn), lambda i, j: (i, j)),
        out_shape=jax.ShapeDtypeStruct((M, N), out_dtype),
        compiler_params=_params("parallel", "arbitrary"),
        name="matmul",
    )(x, w)


def _nt_dot(a, b):
    return lax.dot_general(a, b, (((1,), (1,)), ((), ())), preferred_element_type=F32)


def _na_key_row0(blk, rows):
    return min(max(NA_QROWS * blk - NA_WIN_ROWS // 2, 0), rows - NA_KROWS)


def _na_tile_index(d):
    return min(max(d + NA_WIN_ROWS - 1, -1), 2 * NA_WIN_ROWS - 2) + 1


def _na_kernel(q_ref, k_ref, v_ref, kc_ref, vc_ref, tile_ref, mask_ref, o_ref, *, rows):
    kc = kc_ref[0]
    vc = vc_ref[0]
    nq = NA_QROWS * GRID_W
    for blk in range(rows // NA_QROWS):
        row0 = _na_key_row0(blk, rows)
        k0 = row0 * GRID_W
        q = q_ref[0, blk * nq:(blk + 1) * nq, :]
        kb = k_ref[0, k0:k0 + NA_KROWS * GRID_W, :]
        vb = v_ref[0, k0:k0 + NA_KROWS * GRID_W, :]
        bias = jnp.concatenate([
            jnp.concatenate([tile_ref[0, _na_tile_index(row0 + 2 * jp - (NA_QROWS * blk + a))]
                             for jp in range(NA_KROWS // 2)], axis=1)
            for a in range(NA_QROWS)], axis=0)
        s1 = _nt_dot(q, kb) * ATT_SCALE + bias + mask_ref[blk]
        s2 = _nt_dot(q, kc) * ATT_SCALE
        m = jnp.maximum(jnp.max(s1, axis=-1, keepdims=True), jnp.max(s2, axis=-1, keepdims=True))
        p1 = jnp.exp(s1 - m)
        p2 = jnp.exp(s2 - m)
        l = jnp.sum(p1, axis=-1, keepdims=True) + jnp.sum(p2, axis=-1, keepdims=True)
        o = (jnp.dot(p1.astype(BF16), vb, preferred_element_type=F32)
             + jnp.dot(p2.astype(BF16), vc, preferred_element_type=F32))
        o_ref[0, blk * nq:(blk + 1) * nq, :] = (o / l).astype(o_ref.dtype)


def _na_attention(z3, zc3, tiles, mask):
    B, S, _ = z3.shape
    L = zc3.shape[1]
    rows = S // GRID_W
    assert rows >= NA_KROWS and rows % NA_QROWS == 0
    blk = lambda n, off: pl.BlockSpec((1, n, HEAD_DIM), lambda h, b: (b, 0, off // HEAD_DIM + h))
    return pl.pallas_call(
        functools.partial(_na_kernel, rows=rows),
        grid=(NA_HEADS, B),
        in_specs=[blk(S, OFF_NA_Q), blk(S, OFF_NA_K), blk(S, OFF_NA_V),
                  blk(L, OFF_NA_K), blk(L, OFF_NA_V),
                  pl.BlockSpec((1,) + tiles.shape[1:], lambda h, b: (h, 0, 0, 0)),
                  pl.BlockSpec(mask.shape, lambda h, b: (0, 0, 0))],
        out_specs=pl.BlockSpec((1, S, HEAD_DIM), lambda h, b: (b, 0, h)),
        out_shape=jax.ShapeDtypeStruct((B, S, NA_W), BF16),
        compiler_params=_params("parallel", "parallel"),
        name="na_attention",
    )(z3, z3, z3, zc3, zc3, tiles, mask)


def _na_bias_tiles(rpb):
    col = jnp.arange(GRID_W, dtype=jnp.int32)
    col_idx = jnp.clip(col[None, :] - col[:, None], -(NA_WIN_COLS - 1), NA_WIN_COLS - 1) + NA_WIN_COLS - 1
    t = rpb.astype(F32)[:, :, col_idx]
    t = jnp.pad(t, ((0, 0), (1, 1), (0, 0), (0, 0)))
    return jnp.concatenate([t[:, :-1], t[:, 1:]], axis=-1)


def _na_mask(rows):
    nblk = rows // NA_QROWS
    col = np.arange(GRID_W)
    col_start = np.clip(col - NA_WIN_COLS // 2, 0, GRID_W - NA_WIN_COLS)
    col_ok = (col[None, :] >= col_start[:, None]) & (col[None, :] < col_start[:, None] + NA_WIN_COLS)
    qrow = NA_QROWS * np.arange(nblk)[:, None] + np.arange(NA_QROWS)[None, :]
    krow = np.array([_na_key_row0(b, rows) for b in range(nblk)])[:, None] + np.arange(NA_KROWS)[None, :]
    win0 = np.clip(qrow - NA_WIN_ROWS // 2, 0, rows - NA_WIN_ROWS)[:, :, None]
    row_ok = (krow[:, None, :] >= win0) & (krow[:, None, :] < win0 + NA_WIN_ROWS)
    ok = row_ok[:, :, None, :, None] & col_ok[None, None, :, None, :]
    return jnp.asarray(np.where(ok, 0.0, NEG).astype(np.float32).reshape(
        nblk, NA_QROWS * GRID_W, NA_KROWS * GRID_W))


def _rms(x, g):
    return x * lax.rsqrt(jnp.mean(x * x, axis=-1, keepdims=True) + RMS_EPS) * g


def _rope(x, cos, sin_signed):
    lane = lax.broadcasted_iota(jnp.int32, x.shape, 1)
    nxt = pltpu.roll(x, HEAD_DIM - 1, 1)
    prv = pltpu.roll(x, 1, 1)
    partner = jnp.where((lane & 1) == 0, nxt, prv)
    return x * cos + partner * sin_signed


def _gqa_kernel(q0_ref, q1_ref, q2_ref, k_ref, v_ref, kc_ref, vc_ref, cosk_ref, sink_ref,
                cosq_ref, sinq_ref, qn_ref, kn_ref, o_ref, ks, vs, *, S, L):
    @pl.when(pl.program_id(2) == 0)
    def _():
        k = _rope(_rms(k_ref[0].astype(F32), kn_ref[...]), cosk_ref[...], sink_ref[...])
        ks[0:S, :] = k.astype(BF16)
        ks[S:S + L, :] = _rms(kc_ref[0].astype(F32), kn_ref[...]).astype(BF16)
        vs[0:S, :] = v_ref[0]
        vs[S:S + L, :] = vc_ref[0]

    for g, q_ref in enumerate((q0_ref, q1_ref, q2_ref)):
        q = _rope(_rms(q_ref[0].astype(F32), qn_ref[...]), cosq_ref[...], sinq_ref[...]) * ATT_SCALE
        s = _nt_dot(q.astype(BF16), ks[...])
        m = jnp.max(s, axis=-1, keepdims=True)
        p = jnp.exp(s - m)
        l = jnp.sum(p, axis=-1, keepdims=True)
        o = jnp.dot(p.astype(BF16), vs[...], preferred_element_type=F32) / l
        o_ref[0, :, g * HEAD_DIM:(g + 1) * HEAD_DIM] = o.astype(o_ref.dtype)


def _gqa_attention(z3, zc3, cos, sin, qn, kn):
    B, S, _ = z3.shape
    L = zc3.shape[1]
    tq = _tile(S, 512)
    qspec = lambda g: pl.BlockSpec(
        (1, tq, HEAD_DIM), lambda b, h, i: (b, i, OFF_GQ_Q // HEAD_DIM + h * GQ_GROUP + g))
    kv = lambda n, off: pl.BlockSpec((1, n, HEAD_DIM), lambda b, h, i: (b, 0, off // HEAD_DIM + h))
    full = pl.BlockSpec((S, HEAD_DIM), lambda b, h, i: (0, 0))
    tile = pl.BlockSpec((tq, HEAD_DIM), lambda b, h, i: (i, 0))
    vec = pl.BlockSpec((1, HEAD_DIM), lambda b, h, i: (0, 0))
    return pl.pallas_call(
        functools.partial(_gqa_kernel, S=S, L=L),
        grid=(B, GQ_KV_HEADS, S // tq),
        in_specs=[qspec(0), qspec(1), qspec(2), kv(S, OFF_GQ_K), kv(S, OFF_GQ_V),
                  kv(L, OFF_GQ_K), kv(L, OFF_GQ_V), full, full, tile, tile, vec, vec],
        out_specs=pl.BlockSpec((1, tq, GQ_GROUP * HEAD_DIM), lambda b, h, i: (b, i, h)),
        out_shape=jax.ShapeDtypeStruct((B, S, GQ_W), BF16),
        scratch_shapes=[pltpu.VMEM((S + L, HEAD_DIM), BF16), pltpu.VMEM((S + L, HEAD_DIM), BF16)],
        compiler_params=_params("parallel", "parallel", "arbitrary"),
        name="gqa_attention",
    )(z3, z3, z3, z3, z3, zc3, zc3, cos, sin, cos, sin, qn, kn)


def _rope_tables(S):
    t = jnp.arange(S, dtype=jnp.int32)
    row = (t // GRID_W).astype(F32)
    col = (t % GRID_W).astype(F32)
    axis_dim = HEAD_DIM // 2
    inv_freq = 1.0 / (ROPE_THETA ** (jnp.arange(0, axis_dim, 2, dtype=F32) / axis_dim))
    ang = jnp.concatenate([row[:, None] * inv_freq, col[:, None] * inv_freq], axis=-1)
    cos = jnp.repeat(jnp.cos(ang), 2, axis=-1)
    sin = jnp.repeat(jnp.sin(ang), 2, axis=-1)
    sign = jnp.where(jnp.arange(HEAD_DIM) % 2 == 0, -1.0, 1.0).astype(F32)
    return cos, sin * sign


def _ctx_attn_kernel(q_ref, k_ref, v_ref, qn_ref, kn_ref, o_ref, *, norm):
    q = q_ref[0]
    k = k_ref[0]
    if norm:
        q = _rms(q.astype(F32), qn_ref[...]).astype(BF16)
        k = _rms(k.astype(F32), kn_ref[...]).astype(BF16)
    s = _nt_dot(q, k) * ATT_SCALE
    m = jnp.max(s, axis=-1, keepdims=True)
    p = jnp.exp(s - m)
    l = jnp.sum(p, axis=-1, keepdims=True)
    o_ref[0] = (jnp.dot(p.astype(BF16), v_ref[0], preferred_element_type=F32) / l).astype(o_ref.dtype)


def _ctx_attention(zc3, off_q, off_k, off_v, group, norm, qn, kn):
    B, L, _ = zc3.shape
    n_heads = NA_HEADS
    spec = lambda off, div: pl.BlockSpec((1, L, HEAD_DIM), lambda b, h: (b, 0, off // HEAD_DIM + h // div))
    vec = pl.BlockSpec((1, HEAD_DIM), lambda b, h: (0, 0))
    return pl.pallas_call(
        functools.partial(_ctx_attn_kernel, norm=norm),
        grid=(B, n_heads),
        in_specs=[spec(off_q, 1), spec(off_k, group), spec(off_v, group), vec, vec],
        out_specs=pl.BlockSpec((1, L, HEAD_DIM), lambda b, h: (b, 0, h)),
        out_shape=jax.ShapeDtypeStruct((B, L, n_heads * HEAD_DIM), BF16),
        compiler_params=_params("parallel", "parallel"),
        name="ctx_attention",
    )(zc3, zc3, zc3, qn, kn)


SSM_SLABS = SSM_WIDTH // LANES
SLAB_STATE = SSM_LANES // SSM_SLABS


def _s5_kernel(uf_ref, ub_ref, wb_ref, wc_ref, a_ref, h0_ref, yf_ref, yb_ref, hout_ref,
               br, bi, hs, yt, *, tc, n, nb):
    j = pl.program_id(0)
    rows = tc * nb

    @pl.when(j == 0)
    def _():
        hs[...] = h0_ref[...]

    for d, (u_ref, y_ref) in enumerate(((uf_ref, yf_ref), (ub_ref, yb_ref))):
        u = u_ref[...].reshape(rows, SSM_WIDTH).astype(BF16)
        for k in range(SSM_SLABS):
            r = jnp.dot(u[:, k * LANES:(k + 1) * LANES], wb_ref[d, k], preferred_element_type=F32)
            br[:, k * SLAB_STATE:(k + 1) * SLAB_STATE] = r[:, :SLAB_STATE]
            bi[:, k * SLAB_STATE:(k + 1) * SLAB_STATE] = r[:, SLAB_STATE:]
        ar = jnp.broadcast_to(a_ref[d, 0], (nb, SSM_LANES))
        ai = jnp.broadcast_to(a_ref[d, 1], (nb, SSM_LANES))

        def step(i, carry, d=d, ar=ar, ai=ai):
            hr, hi = carry
            t = i if d == 0 else tc - 1 - i
            row = pl.multiple_of(t * nb, nb)
            nr = ar * hr - ai * hi + br[pl.ds(row, nb), :]
            ni = ar * hi + ai * hr + bi[pl.ds(row, nb), :]
            br[pl.ds(row, nb), :] = nr
            bi[pl.ds(row, nb), :] = ni
            return nr, ni

        hr, hi = lax.fori_loop(0, tc, step, (hs[2 * d], hs[2 * d + 1]))
        hs[2 * d] = hr
        hs[2 * d + 1] = hi
        for k in range(SSM_SLABS):
            sl = slice(k * SLAB_STATE, (k + 1) * SLAB_STATE)
            y = (jnp.dot(br[:, sl].astype(BF16), wc_ref[d, k, 0:SLAB_STATE, :], preferred_element_type=F32)
                 + jnp.dot(bi[:, sl].astype(BF16), wc_ref[d, k, SLAB_STATE:, :], preferred_element_type=F32))
            yt[k] = y
        for b in range(nb):
            for k in range(SSM_SLABS):
                y_ref[b, :, k * LANES:(k + 1) * LANES] = yt[k, pl.ds(b, tc, stride=nb), :]

    @pl.when(j == n - 1)
    def _():
        hout_ref[...] = hs[...]


def _s5_scan(u_t, wb, wc, a, h0):
    T, nb, _ = u_t.shape
    tc = _tile(T, 64)
    n = T // tc
    ublk = lambda rev: pl.BlockSpec((tc, nb, SSM_WIDTH), (lambda j: (n - 1 - j, 0, 0)) if rev else (lambda j: (j, 0, 0)))
    yblk = lambda rev: pl.BlockSpec((nb, tc, SSM_WIDTH), (lambda j: (0, n - 1 - j, 0)) if rev else (lambda j: (0, j, 0)))
    const = lambda shape: pl.BlockSpec(shape, lambda j: (0,) * len(shape))
    return pl.pallas_call(
        functools.partial(_s5_kernel, tc=tc, n=n, nb=nb),
        grid=(n,),
        in_specs=[ublk(False), ublk(True), const(wb.shape), const(wc.shape), const(a.shape), const(h0.shape)],
        out_specs=[yblk(False), yblk(True), const(h0.shape)],
        out_shape=[jax.ShapeDtypeStruct((nb, T, SSM_WIDTH), F32), jax.ShapeDtypeStruct((nb, T, SSM_WIDTH), F32),
                   jax.ShapeDtypeStruct(h0.shape, F32)],
        scratch_shapes=[pltpu.VMEM((tc * nb, SSM_LANES), F32), pltpu.VMEM((tc * nb, SSM_LANES), F32),
                        pltpu.VMEM(h0.shape, F32), pltpu.VMEM((SSM_SLABS, tc * nb, LANES), F32)],
        compiler_params=_params("arbitrary"),
        name="s5_scan",
    )(u_t, u_t, wb, wc, a, h0)


def _s5_weights(lam_r, lam_i, log_dt, b_r, b_i, c_r, c_i):
    lam_r, lam_i, log_dt = lam_r.astype(F32), lam_i.astype(F32), log_dt.astype(F32)
    dt = jnp.exp(log_dt)[..., None]
    mag = jnp.exp(lam_r * dt)
    ar, ai = mag * jnp.cos(lam_i * dt), mag * jnp.sin(lam_i * dt)
    den = lam_r * lam_r + lam_i * lam_i
    nr, ni = ar - 1.0, ai
    kr, ki = (nr * lam_r + ni * lam_i) / den, (ni * lam_r - nr * lam_i) / den
    b_r, b_i = b_r.astype(F32), b_i.astype(F32)
    bbr = kr[..., None] * b_r - ki[..., None] * b_i
    bbi = kr[..., None] * b_i + ki[..., None] * b_r
    gs = SSM_GROUPS // SSM_SLABS
    eye = jnp.eye(gs, dtype=F32)

    def in_slab(w):
        w = w.reshape(2, SSM_SLABS, gs, SSM_STATE, SSM_GROUP)
        return jnp.einsum('dkgph,gj->dkghjp', w, eye).reshape(2, SSM_SLABS, gs * SSM_GROUP, gs * SSM_STATE)

    def out_slab(w):
        w = w.reshape(2, SSM_SLABS, gs, SSM_GROUP, SSM_STATE)
        return jnp.einsum('dkghp,gj->dkjpgh', w, eye).reshape(2, SSM_SLABS, gs * SSM_STATE, gs * SSM_GROUP)

    wb = jnp.concatenate([in_slab(bbr), in_slab(bbi)], axis=-1).astype(BF16)
    wc = jnp.concatenate([out_slab(c_r.astype(F32)), -out_slab(c_i.astype(F32))], axis=-2).astype(BF16)
    a = jnp.stack([ar.reshape(2, 1, SSM_LANES), ai.reshape(2, 1, SSM_LANES)], axis=1)
    return wb, wc, a


def _gelu_tanh(x):
    return 0.5 * x * (1.0 + jnp.tanh(math.sqrt(2.0 / math.pi) * (x + 0.044715 * (x * x * x))))


def _glu_kernel(u_ref, yf_ref, yb_ref, d_ref, w_ref, b_ref, o_ref):
    y = u_ref[0].astype(F32) * d_ref[...] + yf_ref[...] + yb_ref[...]
    t = _gelu_tanh(y).astype(BF16)
    r = jnp.dot(t, w_ref[...], preferred_element_type=F32) + b_ref[...]
    o_ref[0] = (r[:, :SSM_WIDTH] * _sigmoid(r[:, SSM_WIDTH:])).astype(o_ref.dtype)


def _glu(z3, yf, yb, d, w, b):
    B, S, _ = z3.shape
    ts = _tile(S, 512)
    return pl.pallas_call(
        _glu_kernel,
        grid=(B, S // ts),
        in_specs=[pl.BlockSpec((1, ts, SSM_WIDTH), lambda b, i: (b, i, OFF_SU // SSM_WIDTH)),
                  pl.BlockSpec((None, ts, SSM_WIDTH), lambda b, i: (b, i, 0)),
                  pl.BlockSpec((None, ts, SSM_WIDTH), lambda b, i: (b, i, 0)),
                  pl.BlockSpec((1, SSM_WIDTH), lambda b, i: (0, 0)),
                  pl.BlockSpec((SSM_WIDTH, 2 * SSM_WIDTH), lambda b, i: (0, 0)),
                  pl.BlockSpec((1, 2 * SSM_WIDTH), lambda b, i: (0, 0))],
        out_specs=pl.BlockSpec((1, ts, SSM_WIDTH), lambda b, i: (b, i, 0)),
        out_shape=jax.ShapeDtypeStruct((B, S, SSM_WIDTH), BF16),
        compiler_params=_params("parallel", "parallel"),
        name="s5_glu",
    )(z3, yf, yb, d, w, b)


def _merge_kernel(a_ref, b_ref, s_ref, ga_ref, gb_ref, gs_ref, wa_ref, wb_ref, ws_ref, o_ref):
    m = (_sigmoid(ga_ref[...].astype(F32)) * jnp.dot(a_ref[...], wa_ref[...], preferred_element_type=F32)
         + _sigmoid(gb_ref[...].astype(F32)) * jnp.dot(b_ref[...], wb_ref[...], preferred_element_type=F32)
         + _sigmoid(gs_ref[...].astype(F32)) * jnp.dot(s_ref[...], ws_ref[...], preferred_element_type=F32))
    o_ref[...] = m.astype(o_ref.dtype)


def _merge(a, b, s, z2, wa, wb, ws):
    M = a.shape[0]
    D = wa.shape[1]
    tm, tn = _tile(M, 1024), _tile(D, 1024)
    row = lambda w: pl.BlockSpec((tm, w), lambda i, j: (i, 0))
    gate = lambda br: pl.BlockSpec((tm, tn), lambda i, j: (i, (OFF_GATE + br * D) // tn + j))
    wsp = lambda w: pl.BlockSpec((w, tn), lambda i, j: (0, j))
    return pl.pallas_call(
        _merge_kernel,
        grid=(M // tm, D // tn),
        in_specs=[row(NA_W), row(GQ_W), row(SSM_WIDTH), gate(0), gate(1), gate(2),
                  wsp(NA_W), wsp(GQ_W), wsp(SSM_WIDTH)],
        out_specs=pl.BlockSpec((tm, tn), lambda i, j: (i, j)),
        out_shape=jax.ShapeDtypeStruct((M, D), BF16),
        compiler_params=_params("parallel", "arbitrary"),
        name="merge",
    )(a, b, s, z2, z2, z2, wa, wb, ws)


def _layer_norm(x, g, b):
    mu = jnp.mean(x, axis=-1, keepdims=True)
    xc = x - mu
    var = jnp.mean(xc * xc, axis=-1, keepdims=True)
    return xc * lax.rsqrt(var + LN_EPS) * g + b


OUTPROJ_SUBTILES = 2


def _outproj_kernel(m_ref, w_ref, h_ref, g1_ref, sh_ref, sc_ref, lng_ref, lnb_ref, rwh_ref, rwl_ref, rb_ref,
                    cnt0_ref, hn_ref, tokr_ref, tokb_ref, idx_ref, wts_ref, rank_ref, cnt_ref, cnt, *, tm, D):
    @pl.when(pl.program_id(0) == 0)
    def _():
        cnt[...] = cnt0_ref[...]

    logit_parts = []
    for s in range(OUTPROJ_SUBTILES):
        rs = slice(s * tm // OUTPROJ_SUBTILES, (s + 1) * tm // OUTPROJ_SUBTILES)
        mix = jnp.dot(m_ref[rs, :], w_ref[...], preferred_element_type=F32)
        hn = _layer_norm(DN_ALPHA * h_ref[rs, :] + g1_ref[0] * mix, lng_ref[...], lnb_ref[...])
        hn_ref[rs, :] = hn
        tok = hn * (1.0 + sc_ref[0]) + sh_ref[0]
        tokr_ref[rs, :] = tok
        t_hi = tok.astype(BF16)
        tokb_ref[rs, :] = t_hi
        t_lo = (tok - t_hi.astype(F32)).astype(BF16)
        logit_parts.append(_nt_dot(rwh_ref[...], t_hi) + _nt_dot(rwh_ref[...], t_lo) + _nt_dot(rwl_ref[...], t_hi))
    logits = jnp.concatenate(logit_parts, axis=1)
    scores = _sigmoid(logits)
    sel = scores + rb_ref[...]
    eidx = lax.broadcasted_iota(jnp.int32, sel.shape, 0).astype(F32)
    slot = lax.broadcasted_iota(jnp.int32, (TOP_K, tm), 0)
    idx_acc = jnp.zeros((TOP_K, tm), F32)
    w_acc = jnp.zeros((TOP_K, tm), F32)
    hits = []
    for k in range(TOP_K):
        mx = jnp.max(sel, axis=0, keepdims=True)
        am = jnp.min(jnp.where(sel == mx, eidx, float(N_EXPERTS)), axis=0, keepdims=True)
        hit = eidx == am
        hits.append(hit)
        wk = jnp.sum(jnp.where(hit, scores, 0.0), axis=0, keepdims=True)
        idx_acc = jnp.where(slot == k, am, idx_acc)
        w_acc = jnp.where(slot == k, wk, w_acc)
        sel = jnp.where(hit, -jnp.inf, sel)
    wsum = jnp.sum(w_acc, axis=0, keepdims=True)
    idx_ref[...] = idx_acc.astype(jnp.int32)
    wts_ref[...] = w_acc / wsum * ROUTED_SCALE

    picked = jnp.zeros(sel.shape, F32)
    for hit in hits:
        picked = jnp.where(hit, 1.0, picked)
    ri = lax.broadcasted_iota(jnp.int32, (tm, tm), 0)
    ci = lax.broadcasted_iota(jnp.int32, (tm, tm), 1)
    earlier = jnp.where(ri < ci, 1.0, 0.0).astype(BF16)
    before = jnp.dot(picked.astype(BF16), earlier, preferred_element_type=F32) + cnt[...]
    rank_acc = jnp.zeros((TOP_K, tm), F32)
    for k, hit in enumerate(hits):
        rk = jnp.sum(jnp.where(hit, before, 0.0), axis=0, keepdims=True)
        rank_acc = jnp.where(slot == k, rk, rank_acc)
    rank_ref[...] = rank_acc.astype(jnp.int32)
    cnt[...] = cnt[...] + jnp.sum(picked, axis=1, keepdims=True)
    cnt_ref[...] = cnt[...]


def _outproj_ln_route(m, w_out, h2, mod3, row_fn, ln_g, ln_b, rw_hi, rw_lo, rbias, cnt0, tm=256):
    M, D = m.shape
    tm = _tile(M, tm)
    vec = lambda k: pl.BlockSpec((1, 1, D), lambda i: (row_fn(i), 0, k))
    const = lambda shape: pl.BlockSpec(shape, lambda i: (0,) * len(shape))
    rowblk = lambda w: pl.BlockSpec((tm, w), lambda i: (i, 0))
    pick = pl.BlockSpec((TOP_K, tm), lambda i: (0, i))
    return pl.pallas_call(
        functools.partial(_outproj_kernel, tm=tm, D=D),
        grid=(M // tm,),
        in_specs=[rowblk(D), const((D, D)), rowblk(D), vec(2), vec(3), vec(4), const((1, D)), const((1, D)),
                  const((N_EXPERTS, D)), const((N_EXPERTS, D)), const((N_EXPERTS, 1)), const((N_EXPERTS, 1))],
        out_specs=[rowblk(D), rowblk(D), rowblk(D), pick, pick, pick, const((N_EXPERTS, 1))],
        out_shape=[jax.ShapeDtypeStruct((M, D), F32), jax.ShapeDtypeStruct((M, D), F32),
                   jax.ShapeDtypeStruct((M, D), BF16), jax.ShapeDtypeStruct((TOP_K, M), jnp.int32),
                   jax.ShapeDtypeStruct((TOP_K, M), F32), jax.ShapeDtypeStruct((TOP_K, M), jnp.int32),
                   jax.ShapeDtypeStruct((N_EXPERTS, 1), F32)],
        scratch_shapes=[pltpu.VMEM((N_EXPERTS, 1), F32)],
        compiler_params=_params("arbitrary"),
        name="outproj_ln_route",
    )(m, w_out, h2, mod3, mod3, mod3, ln_g, ln_b, rw_hi, rw_lo, rbias, cnt0)


def _swiglu_kernel(x_ref, w13_ref, w2_ref, o_ref, *, E):
    h = jnp.dot(x_ref[...], w13_ref[...], preferred_element_type=F32)
    g = h[:, :E]
    act = (g * _sigmoid(g) * h[:, E:]).astype(BF16)
    o_ref[...] = jnp.dot(act, w2_ref[...], preferred_element_type=F32).astype(o_ref.dtype)


def _shared_expert(tokb, w13, w2):
    T, D = tokb.shape
    E = w2.shape[0]
    tm = _tile(T, 512)
    return pl.pallas_call(
        functools.partial(_swiglu_kernel, E=E),
        grid=(T // tm,),
        in_specs=[pl.BlockSpec((tm, D), lambda i: (i, 0)), pl.BlockSpec((D, 2 * E), lambda i: (0, 0)),
                  pl.BlockSpec((E, D), lambda i: (0, 0))],
        out_specs=pl.BlockSpec((tm, D), lambda i: (i, 0)),
        out_shape=jax.ShapeDtypeStruct((T, D), BF16),
        compiler_params=_params("parallel"),
        name="shared_expert",
    )(tokb, w13, w2)


def _route_plan(idx, rank, counts, tm):
    T = idx.shape[0]
    counts = counts.reshape(N_EXPERTS).astype(jnp.int32)
    padded = (counts + tm - 1) // tm * tm
    pend = jnp.cumsum(padded)
    pstart = pend - padded
    experts = jnp.arange(N_EXPERTS, dtype=jnp.int32)
    dest = (jnp.sum(jnp.where(idx[..., None] == experts, pstart, 0), axis=-1) + rank).astype(jnp.int32)
    n_blk = -(-(T * TOP_K + N_EXPERTS * (tm - 1)) // tm)
    first = jnp.arange(n_blk, dtype=jnp.int32) * tm
    blk_exp = jnp.minimum(jnp.sum(first[:, None] >= pend[None, :], axis=1), N_EXPERTS - 1).astype(jnp.int32)
    n_used = (pend[-1:] // tm).astype(jnp.int32)
    return dest, blk_exp, n_used, (pstart + counts).astype(jnp.int32), (padded - counts).astype(jnp.int32)


def _dispatch_kernel(ps_ref, pc_ref, nu_ref, dest_ref, tok_ref, xs_hbm, zbuf, sem, *, tb, n, tm, n_blk):
    i = pl.program_id(0)

    def pad_rows(do):
        def per_expert(e, c):
            def per_row(r, c2):
                do(pltpu.make_async_copy(zbuf.at[pl.ds(0, 1)], xs_hbm.at[pl.ds(ps_ref[e] + r, 1)], sem.at[1]))
                return c2
            return lax.fori_loop(0, pc_ref[e], per_row, c)
        lax.fori_loop(0, N_EXPERTS, per_expert, 0)

        def per_block(b, c):
            do(pltpu.make_async_copy(zbuf, xs_hbm.at[pl.ds(pl.multiple_of(b * tm, tm), tm)], sem.at[1]))
            return c
        lax.fori_loop(nu_ref[0], n_blk, per_block, 0)

    @pl.when(i == 0)
    def _():
        zbuf[...] = jnp.zeros(zbuf.shape, zbuf.dtype)
        pad_rows(lambda cp: cp.start())

    def body(t, c):
        src = tok_ref.at[pl.ds(t, 1)]
        for k in range(TOP_K):
            d = dest_ref[0, 0, t * TOP_K + k]
            pltpu.make_async_copy(src, xs_hbm.at[pl.ds(d, 1)], sem.at[0]).start()
        return c
    lax.fori_loop(0, tb, body, 0)
    for k in range(TOP_K):
        pltpu.make_async_copy(tok_ref, xs_hbm.at[pl.ds(0, tb)], sem.at[0]).wait()

    @pl.when(i == n - 1)
    def _():
        pad_rows(lambda cp: cp.wait())


def _dispatch(tokr, dest, pad_start, pad_cnt, n_used, n_blk, tm, tb=256):
    T, W = tokr.shape
    tb = _tile(T, tb)
    n = T // tb
    grid_spec = pltpu.PrefetchScalarGridSpec(
        num_scalar_prefetch=3,
        grid=(n,),
        in_specs=[pl.BlockSpec((1, 1, tb * TOP_K), lambda i, ps, pc, nu: (i, 0, 0), memory_space=pltpu.SMEM),
                  pl.BlockSpec((tb, W), lambda i, ps, pc, nu: (i, 0))],
        out_specs=pl.BlockSpec(memory_space=pl.ANY),
        scratch_shapes=[pltpu.VMEM((tm, W), tokr.dtype), pltpu.SemaphoreType.DMA((2,))],
    )
    return pl.pallas_call(
        functools.partial(_dispatch_kernel, tb=tb, n=n, tm=tm, n_blk=n_blk),
        grid_spec=grid_spec,
        out_shape=jax.ShapeDtypeStruct((n_blk * tm, W), tokr.dtype),
        compiler_params=pltpu.CompilerParams(dimension_semantics=("arbitrary",), vmem_limit_bytes=VMEM_LIMIT,
                                             disable_bounds_checks=True),
        name="moe_dispatch",
    )(pad_start, pad_cnt, n_used, dest.reshape(n, 1, tb * TOP_K), tokr)


def _expert_kernel(be_ref, nu_ref, x_ref, w1_ref, w3_ref, w2_ref, y_ref, w13s, w2s, *, E):
    i = pl.program_id(0)

    @pl.when((i == 0) | (be_ref[i] != be_ref[jnp.maximum(i - 1, 0)]))
    def _():
        w13s[:, 0:E] = w1_ref[...].astype(BF16)
        w13s[:, E:2 * E] = w3_ref[...].astype(BF16)
        w2s[...] = w2_ref[...].astype(BF16)

    @pl.when(i < nu_ref[0])
    def _():
        h = jnp.dot(x_ref[...].astype(BF16), w13s[...], preferred_element_type=F32)
        g = h[:, :E]
        act = (g * _sigmoid(g) * h[:, E:]).astype(BF16)
        y_ref[...] = jnp.dot(act, w2s[...], preferred_element_type=F32)

    @pl.when(i >= nu_ref[0])
    def _():
        y_ref[...] = jnp.zeros(y_ref.shape, y_ref.dtype)


def _routed_experts(xs, blk_exp, n_used, w1, w3, w2, l, tm):
    n = blk_exp.shape[0]
    _, _, D, E = w1.shape
    wspec = lambda a, b: pl.BlockSpec((None, None, a, b), lambda i, be, nu: (l, be[i], 0, 0))
    grid_spec = pltpu.PrefetchScalarGridSpec(
        num_scalar_prefetch=2,
        grid=(n,),
        in_specs=[pl.BlockSpec((tm, D), lambda i, be, nu: (jnp.where(i < nu[0], i, 0), 0)),
                  wspec(D, E), wspec(D, E), wspec(E, D)],
        out_specs=pl.BlockSpec((tm, D), lambda i, be, nu: (i, 0)),
        scratch_shapes=[pltpu.VMEM((D, 2 * E), BF16), pltpu.VMEM((E, D), BF16)],
    )
    return pl.pallas_call(
        functools.partial(_expert_kernel, E=E),
        grid_spec=grid_spec,
        out_shape=jax.ShapeDtypeStruct((n * tm, D), F32),
        compiler_params=_params("arbitrary"),
        name="routed_experts",
    )(blk_exp, n_used, xs, w1, w3, w2)


def _combine_kernel(dcur_ref, dnxt_ref, h_ref, sh_ref, w_ref, g2_ref, lng_ref, lnb_ref, ys_hbm, o_ref,
                    ybuf, sem, *, tb, D, n):
    i = pl.program_id(0)
    rows = tb * TOP_K

    def gather(slot, d_ref):
        def body(g, c):
            t0 = pl.multiple_of(g * SUBLANES, SUBLANES)
            dst0 = pl.multiple_of(slot * rows + t0, SUBLANES)
            for u in range(SUBLANES):
                for k in range(TOP_K):
                    d = d_ref[0, 0, (t0 + u) * TOP_K + k]
                    pltpu.make_async_copy(ys_hbm.at[pl.ds(d, 1)], ybuf.at[pl.ds(dst0 + (k * tb + u), 1)],
                                          sem.at[slot]).start()
            return c
        lax.fori_loop(0, tb // SUBLANES, body, 0)

    @pl.when(i == 0)
    def _():
        gather(0, dcur_ref)

    @pl.when(i + 1 < n)
    def _():
        gather((i + 1) % 2, dnxt_ref)

    slot = i % 2
    base = slot * rows
    pltpu.make_async_copy(ys_hbm.at[pl.ds(0, rows)], ybuf.at[pl.ds(pl.multiple_of(base, rows), rows)],
                          sem.at[slot]).wait()
    w = w_ref[...]
    f = sh_ref[...].astype(F32)
    for k in range(TOP_K):
        f = f + w[:, k:k + 1] * ybuf[pl.ds(pl.multiple_of(base + k * tb, tb), tb), :]
    o_ref[...] = _layer_norm(DN_ALPHA * h_ref[...] + g2_ref[0] * f, lng_ref[...], lnb_ref[...])


def _combine_ln(h2, shared, wts, ys, dest, tok_off, mod3, row_fn, ln_g, ln_b, tb=128):
    M, D = h2.shape
    tb = _tile(M, tb)
    n = M // tb
    off = tok_off // tb
    dest3 = dest.reshape(-1, 1, tb * TOP_K)
    rowblk = lambda w, o: pl.BlockSpec((tb, w), lambda i: (i + o, 0))
    const = lambda shape: pl.BlockSpec(shape, lambda i: (0,) * len(shape))
    smem = lambda fn: pl.BlockSpec((1, 1, tb * TOP_K), fn, memory_space=pltpu.SMEM)
    return pl.pallas_call(
        functools.partial(_combine_kernel, tb=tb, D=D, n=n),
        grid=(n,),
        in_specs=[smem(lambda i: (i + off, 0, 0)), smem(lambda i: (jnp.minimum(i + 1, n - 1) + off, 0, 0)),
                  rowblk(D, 0), rowblk(D, off), rowblk(TOP_K, off),
                  pl.BlockSpec((1, 1, D), lambda i: (row_fn(i), 0, 5)), const((1, D)), const((1, D)),
                  pl.BlockSpec(memory_space=pl.ANY)],
        out_specs=rowblk(D, 0),
        out_shape=jax.ShapeDtypeStruct((M, D), F32),
        scratch_shapes=[pltpu.VMEM((2 * tb * TOP_K, D), F32), pltpu.SemaphoreType.DMA((2,))],
        compiler_params=pltpu.CompilerParams(dimension_semantics=("arbitrary",), vmem_limit_bytes=VMEM_LIMIT,
                                             disable_bounds_checks=True),
        name="combine_ln",
    )(dest3, dest3, h2, shared, wts, mod3, ln_g, ln_b, ys)


MOE_TM = 256


def kernel(x, c, ctx, c_ctx, w_mod, b_mod, w_in, na_rpb, gq_q_norm, gq_k_norm, ssm_lam_re, ssm_lam_im, ssm_log_dt, ssm_b_re, ssm_b_im, ssm_c_re, ssm_c_im, ssm_d, ssm_glu_w, ssm_glu_b, w_br_na, w_br_gq, w_br_ssm, w_out, ln1_g, ln1_b, router_w, router_bias, exp_w1, exp_w3, exp_w2, sh_w1, sh_w3, sh_w2, ln2_g, ln2_b):
    B, S, D = x.shape
    L = ctx.shape[1]
    assert B < 2 * SUBLANES and D % LANES == 0
    cos, sin = _rope_tables(S)
    na_mask = _na_mask(S // GRID_W)
    c_all = jnp.zeros((2 * SUBLANES, D), F32).at[:B].set(c).at[B].set(c_ctx)
    lat_row = lambda b: b
    ctx_row = lambda b: B
    h, hc = x, ctx
    for l in range(DEPTH):
        need_ctx = l < DEPTH - 1
        mod3 = _modvec(c_all, w_mod, l, b_mod[l])[:, None, :]
        vec = lambda p: p[l].reshape(1, -1).astype(F32)

        w_in_b = w_in[l].astype(BF16)
        u = _modulate(h, mod3, lat_row, 0, 1)
        uc = _modulate(hc, mod3, ctx_row, 0, 1)
        z2 = _matmul(u.reshape(B * S, D), w_in_b, BF16)
        zc2 = _matmul(uc.reshape(B * L, D), w_in_b if need_ctx else w_in_b[:, :CTX_IN_W], BF16)
        z3, zc3 = z2.reshape(B, S, -1), zc2.reshape(B, L, -1)

        a_out = _na_attention(z3, zc3, _na_bias_tiles(na_rpb[l]), na_mask)
        qn, kn = vec(gq_q_norm), vec(gq_k_norm)
        b_out = _gqa_attention(z3, zc3, cos, sin, qn, kn)

        wb, wc, a = _s5_weights(ssm_lam_re[l], ssm_lam_im[l], ssm_log_dt[l], ssm_b_re[l], ssm_b_im[l],
                                ssm_c_re[l], ssm_c_im[l])
        su = lambda t: t[:, :, OFF_SU:OFF_SU + SSM_WIDTH].astype(F32).transpose(1, 0, 2)
        h0 = jnp.zeros((4, B, SSM_LANES), F32)
        ycf, ycb, hfin = _s5_scan(su(zc3), wb, wc, a, h0)
        yf, yb, _ = _s5_scan(su(z3), wb, wc, a, hfin)
        glu_w, glu_b, dvec = ssm_glu_w[l].astype(BF16), vec(ssm_glu_b), vec(ssm_d)
        c_out = _glu(z3, yf, yb, dvec, glu_w, glu_b)

        wa, wg, ws, wo = (w[l].astype(BF16) for w in (w_br_na, w_br_gq, w_br_ssm, w_out))
        m = _merge(a_out.reshape(B * S, -1), b_out.reshape(B * S, -1), c_out.reshape(B * S, -1), z2, wa, wg, ws)
        rw = router_w[l].astype(F32).T
        rw_hi = rw.astype(BF16)
        rw_lo = (rw - rw_hi.astype(F32)).astype(BF16)
        rbias = router_bias[l].astype(F32).reshape(N_EXPERTS, 1)
        g1, b1 = vec(ln1_g), vec(ln1_b)
        tm_o = _tile(S, 256)
        h2, tokr, tokb, idx, wts, rank, counts = _outproj_ln_route(
            m, wo, h.reshape(B * S, D), mod3, lambda i: i // (S // tm_o), g1, b1, rw_hi, rw_lo, rbias,
            jnp.zeros((N_EXPERTS, 1), F32), tm=tm_o)
        if need_ctx:
            a_ctx = _ctx_attention(zc3, OFF_NA_Q, OFF_NA_K, OFF_NA_V, 1, False, qn, kn)
            b_ctx = _ctx_attention(zc3, OFF_GQ_Q, OFF_GQ_K, OFF_GQ_V, GQ_GROUP, True, qn, kn)
            c_ctx_out = _glu(zc3, ycf, ycb, dvec, glu_w, glu_b)
            mc = _merge(a_ctx.reshape(B * L, -1), b_ctx.reshape(B * L, -1), c_ctx_out.reshape(B * L, -1),
                        zc2, wa, wg, ws)
            tm_c = _tile(L, 256)
            hc2, tokr_c, tokb_c, idx_c, wts_c, rank_c, counts = _outproj_ln_route(
                mc, wo, hc.reshape(B * L, D), mod3, lambda i: B, g1, b1, rw_hi, rw_lo, rbias, counts, tm=tm_c)
            tokr = jnp.concatenate([tokr, tokr_c], axis=0)
            tokb = jnp.concatenate([tokb, tokb_c], axis=0)
            idx = jnp.concatenate([idx, idx_c], axis=1)
            wts = jnp.concatenate([wts, wts_c], axis=1)
            rank = jnp.concatenate([rank, rank_c], axis=1)

        w13 = jnp.concatenate([sh_w1[l], sh_w3[l]], axis=1).astype(BF16)
        shared = _shared_expert(tokb, w13, sh_w2[l].astype(BF16))
        wts = wts.T
        dest, blk_exp, n_used, pad_start, pad_cnt = _route_plan(idx.T, rank.T, counts, MOE_TM)
        xs = _dispatch(tokr, dest, pad_start, pad_cnt, n_used, blk_exp.shape[0], MOE_TM)
        ys = _routed_experts(xs, blk_exp, n_used, exp_w1, exp_w3, exp_w2, l, MOE_TM)
        g2, b2 = vec(ln2_g), vec(ln2_b)
        tb_l = _tile(S, 128)
        h = _combine_ln(h2, shared, wts, ys, dest, 0, mod3, lambda i: i // (S // tb_l), g2, b2,
                        tb=tb_l).reshape(B, S, D)
        if need_ctx:
            hc = _combine_ln(hc2, shared, wts, ys, dest, B * S, mod3, lambda i: B, g2, b2,
                             tb=_tile(L, 128)).reshape(B, L, D)
    return h
```

```python
import functools
import math

import jax
import jax.numpy as jnp
import numpy as np
from jax import lax
from jax.experimental import pallas as pl
from jax.experimental.pallas import tpu as pltpu

F32 = jnp.float32
BF16 = jnp.bfloat16

DEPTH = 2
GRID_W = 64
HEAD_DIM = 128
NA_HEADS = 6
NA_WIN_ROWS = 8
NA_WIN_COLS = 16
GQ_HEADS = 6
GQ_KV_HEADS = 2
GQ_GROUP = GQ_HEADS // GQ_KV_HEADS
ROPE_THETA = 10000.0
SSM_GROUP = 16
SSM_WIDTH = 512
SSM_GROUPS = SSM_WIDTH // SSM_GROUP
SSM_STATE = 64
SSM_LANES = SSM_GROUPS * SSM_STATE
N_EXPERTS = 64
TOP_K = 8
EXPERT_DIM = 512
ROUTED_SCALE = 2.5
DN_ALPHA = (2 * DEPTH) ** 0.25
LN_EPS = 1e-6
RMS_EPS = 1e-6
ATT_SCALE = HEAD_DIM ** -0.5
NEG = -1e30
NA_QROWS = 8
NA_KROWS = 16

LANES = 128
SUBLANES = 8
VMEM_LIMIT = 56 * 1024 * 1024

NA_W = NA_HEADS * HEAD_DIM
GQ_W = GQ_HEADS * HEAD_DIM
GQ_KV_W = GQ_KV_HEADS * HEAD_DIM
OFF_NA_K = 0
OFF_NA_V = OFF_NA_K + NA_W
OFF_GQ_K = OFF_NA_V + NA_W
OFF_GQ_V = OFF_GQ_K + GQ_KV_W
OFF_SU = OFF_GQ_V + GQ_KV_W
CTX_IN_W = OFF_SU + SSM_WIDTH
OFF_NA_Q = CTX_IN_W
OFF_GQ_Q = OFF_NA_Q + NA_W
OFF_GATE = OFF_GQ_Q + GQ_W


def _tile(n, pref):
    t = min(n, pref)
    while n % t:
        t //= 2
    return t


def _params(*sem):
    return pltpu.CompilerParams(dimension_semantics=sem, vmem_limit_bytes=VMEM_LIMIT)


def _sigmoid(x):
    return 1.0 / (1.0 + jnp.exp(-x))


def _modvec_kernel(c_ref, w_ref, b_ref, o_ref):
    c = c_ref[...]
    a = (c * _sigmoid(c)).astype(BF16)
    o_ref[...] = jnp.dot(a, w_ref[...].astype(BF16), preferred_element_type=F32) + b_ref[...]


def _modvec(c_all, w, l, b):
    R, D = c_all.shape
    N = w.shape[2]
    tn = _tile(N, 1024)
    return pl.pallas_call(
        _modvec_kernel,
        grid=(N // tn,),
        in_specs=[pl.BlockSpec((R, D), lambda j: (0, 0)),
                  pl.BlockSpec((None, D, tn), lambda j: (l, 0, j)),
                  pl.BlockSpec((1, tn), lambda j: (0, j))],
        out_specs=pl.BlockSpec((R, tn), lambda j: (0, j)),
        out_shape=jax.ShapeDtypeStruct((R, N), F32),
        compiler_params=_params("arbitrary"),
        name="modvec",
    )(c_all, w, b.reshape(1, N))


def _modulate_kernel(h_ref, sh_ref, sc_ref, o_ref):
    o_ref[0] = (h_ref[0] * (1.0 + sc_ref[0]) + sh_ref[0]).astype(o_ref.dtype)


def _modulate(h, mod3, row_fn, k_shift, k_scale):
    B, S, D = h.shape
    ts = _tile(S, 512)
    return pl.pallas_call(
        _modulate_kernel,
        grid=(B, S // ts),
        in_specs=[pl.BlockSpec((1, ts, D), lambda b, i: (b, i, 0)),
                  pl.BlockSpec((1, 1, D), lambda b, i: (row_fn(b), 0, k_shift)),
                  pl.BlockSpec((1, 1, D), lambda b, i: (row_fn(b), 0, k_scale))],
        out_specs=pl.BlockSpec((1, ts, D), lambda b, i: (b, i, 0)),
        out_shape=jax.ShapeDtypeStruct((B, S, D), BF16),
        compiler_params=_params("parallel", "parallel"),
        name="modulate",
    )(h, mod3, mod3)


def _mm_kernel(x_ref, w_ref, o_ref):
    o_ref[...] = jnp.dot(x_ref[...], w_ref[...], preferred_element_type=F32).astype(o_ref.dtype)


def _matmul(x, w, out_dtype, tm=1024, tn=1024):
    M, K = x.shape
    N = w.shape[1]
    tm, tn = _tile(M, tm), _tile(N, tn)
    return pl.pallas_call(
        _mm_kernel,
        grid=(M // tm, N // tn),
        in_specs=[pl.BlockSpec((tm, K), lambda i, j: (i, 0)),
                  pl.BlockSpec((K, tn), lambda i, j: (0, j))],
        out_specs=pl.BlockSpec((tm, tn), lambda i, j: (i, j)),
        out_shape=jax.ShapeDtypeStruct((M, N), out_dtype),
        compiler_params=_params("parallel", "arbitrary"),
        name="matmul",
    )(x, w)


def _nt_dot(a, b):
    return lax.dot_general(a, b, (((1,), (1,)), ((), ())), preferred_element_type=F32)


def _na_key_row0(blk, rows):
    return min(max(NA_QROWS * blk - NA_WIN_ROWS // 2, 0), rows - NA_KROWS)


def _na_tile_index(d):
    return min(max(d + NA_WIN_ROWS - 1, -1), 2 * NA_WIN_ROWS - 2) + 1


def _na_kernel(q_ref, k_ref, v_ref, kc_ref, vc_ref, tile_ref, mask_ref, o_ref, *, rows):
    kc = kc_ref[0]
    vc = vc_ref[0]
    nq = NA_QROWS * GRID_W
    for blk in range(rows // NA_QROWS):
        row0 = _na_key_row0(blk, rows)
        k0 = row0 * GRID_W
        q = q_ref[0, blk * nq:(blk + 1) * nq, :]
        kb = k_ref[0, k0:k0 + NA_KROWS * GRID_W, :]
        vb = v_ref[0, k0:k0 + NA_KROWS * GRID_W, :]
        bias = jnp.concatenate([
            jnp.concatenate([tile_ref[0, _na_tile_index(row0 + 2 * jp - (NA_QROWS * blk + a))]
                             for jp in range(NA_KROWS // 2)], axis=1)
            for a in range(NA_QROWS)], axis=0)
        s1 = _nt_dot(q, kb) * ATT_SCALE + bias + mask_ref[blk]
        s2 = _nt_dot(q, kc) * ATT_SCALE
        m = jnp.maximum(jnp.max(s1, axis=-1, keepdims=True), jnp.max(s2, axis=-1, keepdims=True))
        p1 = jnp.exp(s1 - m)
        p2 = jnp.exp(s2 - m)
        l = jnp.sum(p1, axis=-1, keepdims=True) + jnp.sum(p2, axis=-1, keepdims=True)
        o = (jnp.dot(p1.astype(BF16), vb, preferred_element_type=F32)
             + jnp.dot(p2.astype(BF16), vc, preferred_element_type=F32))
        o_ref[0, blk * nq:(blk + 1) * nq, :] = (o / l).astype(o_ref.dtype)


def _na_attention(z3, zc3, tiles, mask):
    B, S, _ = z3.shape
    L = zc3.shape[1]
    rows = S // GRID_W
    assert rows >= NA_KROWS and rows % NA_QROWS == 0
    blk = lambda n, off: pl.BlockSpec((1, n, HEAD_DIM), lambda h, b: (b, 0, off // HEAD_DIM + h))
    return pl.pallas_call(
        functools.partial(_na_kernel, rows=rows),
        grid=(NA_HEADS, B),
        in_specs=[blk(S, OFF_NA_Q), blk(S, OFF_NA_K), blk(S, OFF_NA_V),
                  blk(L, OFF_NA_K), blk(L, OFF_NA_V),
                  pl.BlockSpec((1,) + tiles.shape[1:], lambda h, b: (h, 0, 0, 0)),
                  pl.BlockSpec(mask.shape, lambda h, b: (0, 0, 0))],
        out_specs=pl.BlockSpec((1, S, HEAD_DIM), lambda h, b: (b, 0, h)),
        out_shape=jax.ShapeDtypeStruct((B, S, NA_W), BF16),
        compiler_params=_params("parallel", "parallel"),
        name="na_attention",
    )(z3, z3, z3, zc3, zc3, tiles, mask)


def _na_bias_tiles(rpb):
    col = jnp.arange(GRID_W, dtype=jnp.int32)
    col_idx = jnp.clip(col[None, :] - col[:, None], -(NA_WIN_COLS - 1), NA_WIN_COLS - 1) + NA_WIN_COLS - 1
    t = rpb.astype(F32)[:, :, col_idx]
    t = jnp.pad(t, ((0, 0), (1, 1), (0, 0), (0, 0)))
    return jnp.concatenate([t[:, :-1], t[:, 1:]], axis=-1)


def _na_mask(rows):
    nblk = rows // NA_QROWS
    col = np.arange(GRID_W)
    col_start = np.clip(col - NA_WIN_COLS // 2, 0, GRID_W - NA_WIN_COLS)
    col_ok = (col[None, :] >= col_start[:, None]) & (col[None, :] < col_start[:, None] + NA_WIN_COLS)
    qrow = NA_QROWS * np.arange(nblk)[:, None] + np.arange(NA_QROWS)[None, :]
    krow = np.array([_na_key_row0(b, rows) for b in range(nblk)])[:, None] + np.arange(NA_KROWS)[None, :]
    win0 = np.clip(qrow - NA_WIN_ROWS // 2, 0, rows - NA_WIN_ROWS)[:, :, None]
    row_ok = (krow[:, None, :] >= win0) & (krow[:, None, :] < win0 + NA_WIN_ROWS)
    ok = row_ok[:, :, None, :, None] & col_ok[None, None, :, None, :]
    return jnp.asarray(np.where(ok, 0.0, NEG).astype(np.float32).reshape(
        nblk, NA_QROWS * GRID_W, NA_KROWS * GRID_W))


def _rms(x, g):
    return x * lax.rsqrt(jnp.mean(x * x, axis=-1, keepdims=True) + RMS_EPS) * g


def _rope(x, cos, sin_signed):
    lane = lax.broadcasted_iota(jnp.int32, x.shape, 1)
    nxt = pltpu.roll(x, HEAD_DIM - 1, 1)
    prv = pltpu.roll(x, 1, 1)
    partner = jnp.where((lane & 1) == 0, nxt, prv)
    return x * cos + partner * sin_signed


def _gqa_kernel(q0_ref, q1_ref, q2_ref, k_ref, v_ref, kc_ref, vc_ref, cosk_ref, sink_ref,
                cosq_ref, sinq_ref, qn_ref, kn_ref, o_ref, ks, vs, *, S, L):
    @pl.when(pl.program_id(2) == 0)
    def _():
        k = _rope(_rms(k_ref[0].astype(F32), kn_ref[...]), cosk_ref[...], sink_ref[...])
        ks[0:S, :] = k.astype(BF16)
        ks[S:S + L, :] = _rms(kc_ref[0].astype(F32), kn_ref[...]).astype(BF16)
        vs[0:S, :] = v_ref[0]
        vs[S:S + L, :] = vc_ref[0]

    for g, q_ref in enumerate((q0_ref, q1_ref, q2_ref)):
        q = _rope(_rms(q_ref[0].astype(F32), qn_ref[...]), cosq_ref[...], sinq_ref[...]) * ATT_SCALE
        s = _nt_dot(q.astype(BF16), ks[...])
        m = jnp.max(s, axis=-1, keepdims=True)
        p = jnp.exp(s - m)
        l = jnp.sum(p, axis=-1, keepdims=True)
        o = jnp.dot(p.astype(BF16), vs[...], preferred_element_type=F32) / l
        o_ref[0, :, g * HEAD_DIM:(g + 1) * HEAD_DIM] = o.astype(o_ref.dtype)


def _gqa_attention(z3, zc3, cos, sin, qn, kn):
    B, S, _ = z3.shape
    L = zc3.shape[1]
    tq = _tile(S, 512)
    qspec = lambda g: pl.BlockSpec(
        (1, tq, HEAD_DIM), lambda b, h, i: (b, i, OFF_GQ_Q // HEAD_DIM + h * GQ_GROUP + g))
    kv = lambda n, off: pl.BlockSpec((1, n, HEAD_DIM), lambda b, h, i: (b, 0, off // HEAD_DIM + h))
    full = pl.BlockSpec((S, HEAD_DIM), lambda b, h, i: (0, 0))
    tile = pl.BlockSpec((tq, HEAD_DIM), lambda b, h, i: (i, 0))
    vec = pl.BlockSpec((1, HEAD_DIM), lambda b, h, i: (0, 0))
    return pl.pallas_call(
        functools.partial(_gqa_kernel, S=S, L=L),
        grid=(B, GQ_KV_HEADS, S // tq),
        in_specs=[qspec(0), qspec(1), qspec(2), kv(S, OFF_GQ_K), kv(S, OFF_GQ_V),
                  kv(L, OFF_GQ_K), kv(L, OFF_GQ_V), full, full, tile, tile, vec, vec],
        out_specs=pl.BlockSpec((1, tq, GQ_GROUP * HEAD_DIM), lambda b, h, i: (b, i, h)),
        out_shape=jax.ShapeDtypeStruct((B, S, GQ_W), BF16),
        scratch_shapes=[pltpu.VMEM((S + L, HEAD_DIM), BF16), pltpu.VMEM((S + L, HEAD_DIM), BF16)],
        compiler_params=_params("parallel", "parallel", "arbitrary"),
        name="gqa_attention",
    )(z3, z3, z3, z3, z3, zc3, zc3, cos, sin, cos, sin, qn, kn)


def _rope_tables(S):
    t = jnp.arange(S, dtype=jnp.int32)
    row = (t // GRID_W).astype(F32)
    col = (t % GRID_W).astype(F32)
    axis_dim = HEAD_DIM // 2
    inv_freq = 1.0 / (ROPE_THETA ** (jnp.arange(0, axis_dim, 2, dtype=F32) / axis_dim))
    ang = jnp.concatenate([row[:, None] * inv_freq, col[:, None] * inv_freq], axis=-1)
    cos = jnp.repeat(jnp.cos(ang), 2, axis=-1)
    sin = jnp.repeat(jnp.sin(ang), 2, axis=-1)
    sign = jnp.where(jnp.arange(HEAD_DIM) % 2 == 0, -1.0, 1.0).astype(F32)
    return cos, sin * sign


def _ctx_attn_kernel(q_ref, k_ref, v_ref, qn_ref, kn_ref, o_ref, *, norm):
    q = q_ref[0]
    k = k_ref[0]
    if norm:
        q = _rms(q.astype(F32), qn_ref[...]).astype(BF16)
        k = _rms(k.astype(F32), kn_ref[...]).astype(BF16)
    s = _nt_dot(q, k) * ATT_SCALE
    m = jnp.max(s, axis=-1, keepdims=True)
    p = jnp.exp(s - m)
    l = jnp.sum(p, axis=-1, keepdims=True)
    o_ref[0] = (jnp.dot(p.astype(BF16), v_ref[0], preferred_element_type=F32) / l).astype(o_ref.dtype)


def _ctx_attention(zc3, off_q, off_k, off_v, group, norm, qn, kn):
    B, L, _ = zc3.shape
    n_heads = NA_HEADS
    spec = lambda off, div: pl.BlockSpec((1, L, HEAD_DIM), lambda b, h: (b, 0, off // HEAD_DIM + h // div))
    vec = pl.BlockSpec((1, HEAD_DIM), lambda b, h: (0, 0))
    return pl.pallas_call(
        functools.partial(_ctx_attn_kernel, norm=norm),
        grid=(B, n_heads),
        in_specs=[spec(off_q, 1), spec(off_k, group), spec(off_v, group), vec, vec],
        out_specs=pl.BlockSpec((1, L, HEAD_DIM), lambda b, h: (b, 0, h)),
        out_shape=jax.ShapeDtypeStruct((B, L, n_heads * HEAD_DIM), BF16),
        compiler_params=_params("parallel", "parallel"),
        name="ctx_attention",
    )(zc3, zc3, zc3, qn, kn)


SSM_SLABS = SSM_WIDTH // LANES
SLAB_STATE = SSM_LANES // SSM_SLABS


def _s5_kernel(uf_ref, ub_ref, wb_ref, wc_ref, a_ref, h0_ref, yf_ref, yb_ref, hout_ref,
               br, bi, hs, yt, *, tc, n, nb):
    j = pl.program_id(0)
    rows = tc * nb

    @pl.when(j == 0)
    def _():
        hs[...] = h0_ref[...]

    for d, (u_ref, y_ref) in enumerate(((uf_ref, yf_ref), (ub_ref, yb_ref))):
        u = u_ref[...].reshape(rows, SSM_WIDTH).astype(BF16)
        for k in range(SSM_SLABS):
            r = jnp.dot(u[:, k * LANES:(k + 1) * LANES], wb_ref[d, k], preferred_element_type=F32)
            br[:, k * SLAB_STATE:(k + 1) * SLAB_STATE] = r[:, :SLAB_STATE]
            bi[:, k * SLAB_STATE:(k + 1) * SLAB_STATE] = r[:, SLAB_STATE:]
        ar = jnp.broadcast_to(a_ref[d, 0], (nb, SSM_LANES))
        ai = jnp.broadcast_to(a_ref[d, 1], (nb, SSM_LANES))

        def step(i, carry, d=d, ar=ar, ai=ai):
            hr, hi = carry
            t = i if d == 0 else tc - 1 - i
            row = pl.multiple_of(t * nb, nb)
            nr = ar * hr - ai * hi + br[pl.ds(row, nb), :]
            ni = ar * hi + ai * hr + bi[pl.ds(row, nb), :]
            br[pl.ds(row, nb), :] = nr
            bi[pl.ds(row, nb), :] = ni
            return nr, ni

        hr, hi = lax.fori_loop(0, tc, step, (hs[2 * d], hs[2 * d + 1]))
        hs[2 * d] = hr
        hs[2 * d + 1] = hi
        for k in range(SSM_SLABS):
            sl = slice(k * SLAB_STATE, (k + 1) * SLAB_STATE)
            y = (jnp.dot(br[:, sl].astype(BF16), wc_ref[d, k, 0:SLAB_STATE, :], preferred_element_type=F32)
                 + jnp.dot(bi[:, sl].astype(BF16), wc_ref[d, k, SLAB_STATE:, :], preferred_element_type=F32))
            yt[k] = y
        for b in range(nb):
            for k in range(SSM_SLABS):
                y_ref[b, :, k * LANES:(k + 1) * LANES] = yt[k, pl.ds(b, tc, stride=nb), :]

    @pl.when(j == n - 1)
    def _():
        hout_ref[...] = hs[...]


def _s5_scan(u_t, wb, wc, a, h0):
    T, nb, _ = u_t.shape
    tc = _tile(T, 64)
    n = T // tc
    ublk = lambda rev: pl.BlockSpec((tc, nb, SSM_WIDTH), (lambda j: (n - 1 - j, 0, 0)) if rev else (lambda j: (j, 0, 0)))
    yblk = lambda rev: pl.BlockSpec((nb, tc, SSM_WIDTH), (lambda j: (0, n - 1 - j, 0)) if rev else (lambda j: (0, j, 0)))
    const = lambda shape: pl.BlockSpec(shape, lambda j: (0,) * len(shape))
    return pl.pallas_call(
        functools.partial(_s5_kernel, tc=tc, n=n, nb=nb),
        grid=(n,),
        in_specs=[ublk(False), ublk(True), const(wb.shape), const(wc.shape), const(a.shape), const(h0.shape)],
        out_specs=[yblk(False), yblk(True), const(h0.shape)],
        out_shape=[jax.ShapeDtypeStruct((nb, T, SSM_WIDTH), F32), jax.ShapeDtypeStruct((nb, T, SSM_WIDTH), F32),
                   jax.ShapeDtypeStruct(h0.shape, F32)],
        scratch_shapes=[pltpu.VMEM((tc * nb, SSM_LANES), F32), pltpu.VMEM((tc * nb, SSM_LANES), F32),
                        pltpu.VMEM(h0.shape, F32), pltpu.VMEM((SSM_SLABS, tc * nb, LANES), F32)],
        compiler_params=_params("arbitrary"),
        name="s5_scan",
    )(u_t, u_t, wb, wc, a, h0)


def _s5_weights(lam_r, lam_i, log_dt, b_r, b_i, c_r, c_i):
    lam_r, lam_i, log_dt = lam_r.astype(F32), lam_i.astype(F32), log_dt.astype(F32)
    dt = jnp.exp(log_dt)[..., None]
    mag = jnp.exp(lam_r * dt)
    ar, ai = mag * jnp.cos(lam_i * dt), mag * jnp.sin(lam_i * dt)
    den = lam_r * lam_r + lam_i * lam_i
    nr, ni = ar - 1.0, ai
    kr, ki = (nr * lam_r + ni * lam_i) / den, (ni * lam_r - nr * lam_i) / den
    b_r, b_i = b_r.astype(F32), b_i.astype(F32)
    bbr = kr[..., None] * b_r - ki[..., None] * b_i
    bbi = kr[..., None] * b_i + ki[..., None] * b_r
    gs = SSM_GROUPS // SSM_SLABS
    eye = jnp.eye(gs, dtype=F32)

    def in_slab(w):
        w = w.reshape(2, SSM_SLABS, gs, SSM_STATE, SSM_GROUP)
        return jnp.einsum('dkgph,gj->dkghjp', w, eye).reshape(2, SSM_SLABS, gs * SSM_GROUP, gs * SSM_STATE)

    def out_slab(w):
        w = w.reshape(2, SSM_SLABS, gs, SSM_GROUP, SSM_STATE)
        return jnp.einsum('dkghp,gj->dkjpgh', w, eye).reshape(2, SSM_SLABS, gs * SSM_STATE, gs * SSM_GROUP)

    wb = jnp.concatenate([in_slab(bbr), in_slab(bbi)], axis=-1).astype(BF16)
    wc = jnp.concatenate([out_slab(c_r.astype(F32)), -out_slab(c_i.astype(F32))], axis=-2).astype(BF16)
    a = jnp.stack([ar.reshape(2, 1, SSM_LANES), ai.reshape(2, 1, SSM_LANES)], axis=1)
    return wb, wc, a


def _gelu_tanh(x):
    return 0.5 * x * (1.0 + jnp.tanh(math.sqrt(2.0 / math.pi) * (x + 0.044715 * (x * x * x))))


def _glu_kernel(u_ref, yf_ref, yb_ref, d_ref, w_ref, b_ref, o_ref):
    y = u_ref[0].astype(F32) * d_ref[...] + yf_ref[...] + yb_ref[...]
    t = _gelu_tanh(y).astype(BF16)
    r = jnp.dot(t, w_ref[...], preferred_element_type=F32) + b_ref[...]
    o_ref[0] = (r[:, :SSM_WIDTH] * _sigmoid(r[:, SSM_WIDTH:])).astype(o_ref.dtype)


def _glu(z3, yf, yb, d, w, b):
    B, S, _ = z3.shape
    ts = _tile(S, 512)
    return pl.pallas_call(
        _glu_kernel,
        grid=(B, S // ts),
        in_specs=[pl.BlockSpec((1, ts, SSM_WIDTH), lambda b, i: (b, i, OFF_SU // SSM_WIDTH)),
                  pl.BlockSpec((None, ts, SSM_WIDTH), lambda b, i: (b, i, 0)),
                  pl.BlockSpec((None, ts, SSM_WIDTH), lambda b, i: (b, i, 0)),
                  pl.BlockSpec((1, SSM_WIDTH), lambda b, i: (0, 0)),
                  pl.BlockSpec((SSM_WIDTH, 2 * SSM_WIDTH), lambda b, i: (0, 0)),
                  pl.BlockSpec((1, 2 * SSM_WIDTH), lambda b, i: (0, 0))],
        out_specs=pl.BlockSpec((1, ts, SSM_WIDTH), lambda b, i: (b, i, 0)),
        out_shape=jax.ShapeDtypeStruct((B, S, SSM_WIDTH), BF16),
        compiler_params=_params("parallel", "parallel"),
        name="s5_glu",
    )(z3, yf, yb, d, w, b)


def _merge_kernel(a_ref, b_ref, s_ref, ga_ref, gb_ref, gs_ref, wa_ref, wb_ref, ws_ref, o_ref):
    m = (_sigmoid(ga_ref[...].astype(F32)) * jnp.dot(a_ref[...], wa_ref[...], preferred_element_type=F32)
         + _sigmoid(gb_ref[...].astype(F32)) * jnp.dot(b_ref[...], wb_ref[...], preferred_element_type=F32)
         + _sigmoid(gs_ref[...].astype(F32)) * jnp.dot(s_ref[...], ws_ref[...], preferred_element_type=F32))
    o_ref[...] = m.astype(o_ref.dtype)


def _merge(a, b, s, z2, wa, wb, ws):
    M = a.shape[0]
    D = wa.shape[1]
    tm, tn = _tile(M, 1024), _tile(D, 1024)
    row = lambda w: pl.BlockSpec((tm, w), lambda i, j: (i, 0))
    gate = lambda br: pl.BlockSpec((tm, tn), lambda i, j: (i, (OFF_GATE + br * D) // tn + j))
    wsp = lambda w: pl.BlockSpec((w, tn), lambda i, j: (0, j))
    return pl.pallas_call(
        _merge_kernel,
        grid=(M // tm, D // tn),
        in_specs=[row(NA_W), row(GQ_W), row(SSM_WIDTH), gate(0), gate(1), gate(2),
                  wsp(NA_W), wsp(GQ_W), wsp(SSM_WIDTH)],
        out_specs=pl.BlockSpec((tm, tn), lambda i, j: (i, j)),
        out_shape=jax.ShapeDtypeStruct((M, D), BF16),
        compiler_params=_params("parallel", "arbitrary"),
        name="merge",
    )(a, b, s, z2, z2, z2, wa, wb, ws)


def _layer_norm(x, g, b):
    mu = jnp.mean(x, axis=-1, keepdims=True)
    xc = x - mu
    var = jnp.mean(xc * xc, axis=-1, keepdims=True)
    return xc * lax.rsqrt(var + LN_EPS) * g + b


OUTPROJ_SUBTILES = 2


def _outproj_kernel(m_ref, w_ref, h_ref, g1_ref, sh_ref, sc_ref, lng_ref, lnb_ref, rwh_ref, rwl_ref, rb_ref,
                    cnt0_ref, hn_ref, tokr_ref, tokb_ref, idx_ref, wts_ref, rank_ref, cnt_ref, cnt, *, tm, D):
    @pl.when(pl.program_id(0) == 0)
    def _():
        cnt[...] = cnt0_ref[...]

    logit_parts = []
    for s in range(OUTPROJ_SUBTILES):
        rs = slice(s * tm // OUTPROJ_SUBTILES, (s + 1) * tm // OUTPROJ_SUBTILES)
        mix = jnp.dot(m_ref[rs, :], w_ref[...], preferred_element_type=F32)
        hn = _layer_norm(DN_ALPHA * h_ref[rs, :] + g1_ref[0] * mix, lng_ref[...], lnb_ref[...])
        hn_ref[rs, :] = hn
        tok = hn * (1.0 + sc_ref[0]) + sh_ref[0]
        tokr_ref[rs, :] = tok
        t_hi = tok.astype(BF16)
        tokb_ref[rs, :] = t_hi
        t_lo = (tok - t_hi.astype(F32)).astype(BF16)
        logit_parts.append(_nt_dot(rwh_ref[...], t_hi) + _nt_dot(rwh_ref[...], t_lo) + _nt_dot(rwl_ref[...], t_hi))
    logits = jnp.concatenate(logit_parts, axis=1)
    scores = _sigmoid(logits)
    sel = scores + rb_ref[...]
    eidx = lax.broadcasted_iota(jnp.int32, sel.shape, 0).astype(F32)
    slot = lax.broadcasted_iota(jnp.int32, (TOP_K, tm), 0)
    idx_acc = jnp.zeros((TOP_K, tm), F32)
    w_acc = jnp.zeros((TOP_K, tm), F32)
    hits = []
    for k in range(TOP_K):
        mx = jnp.max(sel, axis=0, keepdims=True)
        am = jnp.min(jnp.where(sel == mx, eidx, float(N_EXPERTS)), axis=0, keepdims=True)
        hit = eidx == am
        hits.append(hit)
        wk = jnp.sum(jnp.where(hit, scores, 0.0), axis=0, keepdims=True)
        idx_acc = jnp.where(slot == k, am, idx_acc)
        w_acc = jnp.where(slot == k, wk, w_acc)
        sel = jnp.where(hit, -jnp.inf, sel)
    wsum = jnp.sum(w_acc, axis=0, keepdims=True)
    idx_ref[...] = idx_acc.astype(jnp.int32)
    wts_ref[...] = w_acc / wsum * ROUTED_SCALE

    picked = jnp.zeros(sel.shape, F32)
    for hit in hits:
        picked = jnp.where(hit, 1.0, picked)
    ri = lax.broadcasted_iota(jnp.int32, (tm, tm), 0)
    ci = lax.broadcasted_iota(jnp.int32, (tm, tm), 1)
    earlier = jnp.where(ri < ci, 1.0, 0.0).astype(BF16)
    before = jnp.dot(picked.astype(BF16), earlier, preferred_element_type=F32) + cnt[...]
    rank_acc = jnp.zeros((TOP_K, tm), F32)
    for k, hit in enumerate(hits):
        rk = jnp.sum(jnp.where(hit, before, 0.0), axis=0, keepdims=True)
        rank_acc = jnp.where(slot == k, rk, rank_acc)
    rank_ref[...] = rank_acc.astype(jnp.int32)
    cnt[...] = cnt[...] + jnp.sum(picked, axis=1, keepdims=True)
    cnt_ref[...] = cnt[...]


def _outproj_ln_route(m, w_out, h2, mod3, row_fn, ln_g, ln_b, rw_hi, rw_lo, rbias, cnt0, tm=256):
    M, D = m.shape
    tm = _tile(M, tm)
    vec = lambda k: pl.BlockSpec((1, 1, D), lambda i: (row_fn(i), 0, k))
    const = lambda shape: pl.BlockSpec(shape, lambda i: (0,) * len(shape))
    rowblk = lambda w: pl.BlockSpec((tm, w), lambda i: (i, 0))
    pick = pl.BlockSpec((TOP_K, tm), lambda i: (0, i))
    return pl.pallas_call(
        functools.partial(_outproj_kernel, tm=tm, D=D),
        grid=(M // tm,),
        in_specs=[rowblk(D), const((D, D)), rowblk(D), vec(2), vec(3), vec(4), const((1, D)), const((1, D)),
                  const((N_EXPERTS, D)), const((N_EXPERTS, D)), const((N_EXPERTS, 1)), const((N_EXPERTS, 1))],
        out_specs=[rowblk(D), rowblk(D), rowblk(D), pick, pick, pick, const((N_EXPERTS, 1))],
        out_shape=[jax.ShapeDtypeStruct((M, D), F32), jax.ShapeDtypeStruct((M, D), F32),
                   jax.ShapeDtypeStruct((M, D), BF16), jax.ShapeDtypeStruct((TOP_K, M), jnp.int32),
                   jax.ShapeDtypeStruct((TOP_K, M), F32), jax.ShapeDtypeStruct((TOP_K, M), jnp.int32),
                   jax.ShapeDtypeStruct((N_EXPERTS, 1), F32)],
        scratch_shapes=[pltpu.VMEM((N_EXPERTS, 1), F32)],
        compiler_params=_params("arbitrary"),
        name="outproj_ln_route",
    )(m, w_out, h2, mod3, mod3, mod3, ln_g, ln_b, rw_hi, rw_lo, rbias, cnt0)


def _swiglu_kernel(x_ref, w13_ref, w2_ref, o_ref, *, E):
    h = jnp.dot(x_ref[...], w13_ref[...], preferred_element_type=F32)
    g = h[:, :E]
    act = (g * _sigmoid(g) * h[:, E:]).astype(BF16)
    o_ref[...] = jnp.dot(act, w2_ref[...], preferred_element_type=F32).astype(o_ref.dtype)


def _shared_expert(tokb, w13, w2):
    T, D = tokb.shape
    E = w2.shape[0]
    tm = _tile(T, 512)
    return pl.pallas_call(
        functools.partial(_swiglu_kernel, E=E),
        grid=(T // tm,),
        in_specs=[pl.BlockSpec((tm, D), lambda i: (i, 0)), pl.BlockSpec((D, 2 * E), lambda i: (0, 0)),
                  pl.BlockSpec((E, D), lambda i: (0, 0))],
        out_specs=pl.BlockSpec((tm, D), lambda i: (i, 0)),
        out_shape=jax.ShapeDtypeStruct((T, D), BF16),
        compiler_params=_params("parallel"),
        name="shared_expert",
    )(tokb, w13, w2)


def _route_plan(idx, rank, counts, tm):
    T = idx.shape[0]
    counts = counts.reshape(N_EXPERTS).astype(jnp.int32)
    padded = (counts + tm - 1) // tm * tm
    pend = jnp.cumsum(padded)
    pstart = pend - padded
    experts = jnp.arange(N_EXPERTS, dtype=jnp.int32)
    dest = (jnp.sum(jnp.where(idx[..., None] == experts, pstart, 0), axis=-1) + rank).astype(jnp.int32)
    n_blk = -(-(T * TOP_K + N_EXPERTS * (tm - 1)) // tm)
    first = jnp.arange(n_blk, dtype=jnp.int32) * tm
    blk_exp = jnp.minimum(jnp.sum(first[:, None] >= pend[None, :], axis=1), N_EXPERTS - 1).astype(jnp.int32)
    n_used = (pend[-1:] // tm).astype(jnp.int32)
    return dest, blk_exp, n_used, (pstart + counts).astype(jnp.int32), (padded - counts).astype(jnp.int32)


def _dispatch_kernel(ps_ref, pc_ref, nu_ref, dest_ref, tok_ref, xs_hbm, zbuf, sem, *, tb, n, tm, n_blk):
    i = pl.program_id(0)

    def pad_rows(do):
        def per_expert(e, c):
            def per_row(r, c2):
                do(pltpu.make_async_copy(zbuf.at[pl.ds(0, 1)], xs_hbm.at[pl.ds(ps_ref[e] + r, 1)], sem.at[1]))
                return c2
            return lax.fori_loop(0, pc_ref[e], per_row, c)
        lax.fori_loop(0, N_EXPERTS, per_expert, 0)

        def per_block(b, c):
            do(pltpu.make_async_copy(zbuf, xs_hbm.at[pl.ds(pl.multiple_of(b * tm, tm), tm)], sem.at[1]))
            return c
        lax.fori_loop(nu_ref[0], n_blk, per_block, 0)

    @pl.when(i == 0)
    def _():
        zbuf[...] = jnp.zeros(zbuf.shape, zbuf.dtype)
        pad_rows(lambda cp: cp.start())

    def body(t, c):
        src = tok_ref.at[pl.ds(t, 1)]
        for k in range(TOP_K):
            d = dest_ref[0, 0, t * TOP_K + k]
            pltpu.make_async_copy(src, xs_hbm.at[pl.ds(d, 1)], sem.at[0]).start(priority=k % 2)
        return c
    lax.fori_loop(0, tb, body, 0)
    for k in range(TOP_K):
        pltpu.make_async_copy(tok_ref, xs_hbm.at[pl.ds(0, tb)], sem.at[0]).wait()

    @pl.when(i == n - 1)
    def _():
        pad_rows(lambda cp: cp.wait())


def _dispatch(tokr, dest, pad_start, pad_cnt, n_used, n_blk, tm, tb=256):
    T, W = tokr.shape
    tb = _tile(T, tb)
    n = T // tb
    grid_spec = pltpu.PrefetchScalarGridSpec(
        num_scalar_prefetch=3,
        grid=(n,),
        in_specs=[pl.BlockSpec((1, 1, tb * TOP_K), lambda i, ps, pc, nu: (i, 0, 0), memory_space=pltpu.SMEM),
                  pl.BlockSpec((tb, W), lambda i, ps, pc, nu: (i, 0))],
        out_specs=pl.BlockSpec(memory_space=pl.ANY),
        scratch_shapes=[pltpu.VMEM((tm, W), tokr.dtype), pltpu.SemaphoreType.DMA((2,))],
    )
    return pl.pallas_call(
        functools.partial(_dispatch_kernel, tb=tb, n=n, tm=tm, n_blk=n_blk),
        grid_spec=grid_spec,
        out_shape=jax.ShapeDtypeStruct((n_blk * tm, W), tokr.dtype),
        compiler_params=pltpu.CompilerParams(dimension_semantics=("arbitrary",), vmem_limit_bytes=VMEM_LIMIT,
                                             disable_bounds_checks=True),
        name="moe_dispatch",
    )(pad_start, pad_cnt, n_used, dest.reshape(n, 1, tb * TOP_K), tokr)


def _expert_kernel(be_ref, nu_ref, x_ref, w1_ref, w3_ref, w2_ref, y_ref, w13s, w2s, *, E):
    i = pl.program_id(0)

    @pl.when((i == 0) | (be_ref[i] != be_ref[jnp.maximum(i - 1, 0)]))
    def _():
        w13s[:, 0:E] = w1_ref[...].astype(BF16)
        w13s[:, E:2 * E] = w3_ref[...].astype(BF16)
        w2s[...] = w2_ref[...].astype(BF16)

    @pl.when(i < nu_ref[0])
    def _():
        h = jnp.dot(x_ref[...].astype(BF16), w13s[...], preferred_element_type=F32)
        g = h[:, :E]
        act = (g * _sigmoid(g) * h[:, E:]).astype(BF16)
        y_ref[...] = jnp.dot(act, w2s[...], preferred_element_type=F32)

    @pl.when(i >= nu_ref[0])
    def _():
        y_ref[...] = jnp.zeros(y_ref.shape, y_ref.dtype)


def _routed_experts(xs, blk_exp, n_used, w1, w3, w2, l, tm):
    n = blk_exp.shape[0]
    _, _, D, E = w1.shape
    wspec = lambda a, b: pl.BlockSpec((None, None, a, b), lambda i, be, nu: (l, be[i], 0, 0))
    grid_spec = pltpu.PrefetchScalarGridSpec(
        num_scalar_prefetch=2,
        grid=(n,),
        in_specs=[pl.BlockSpec((tm, D), lambda i, be, nu: (jnp.where(i < nu[0], i, 0), 0)),
                  wspec(D, E), wspec(D, E), wspec(E, D)],
        out_specs=pl.BlockSpec((tm, D), lambda i, be, nu: (i, 0)),
        scratch_shapes=[pltpu.VMEM((D, 2 * E), BF16), pltpu.VMEM((E, D), BF16)],
    )
    return pl.pallas_call(
        functools.partial(_expert_kernel, E=E),
        grid_spec=grid_spec,
        out_shape=jax.ShapeDtypeStruct((n * tm, D), F32),
        compiler_params=_params("arbitrary"),
        name="routed_experts",
    )(blk_exp, n_used, xs, w1, w3, w2)


def _combine_kernel(dcur_ref, dnxt_ref, h_ref, sh_ref, w_ref, g2_ref, lng_ref, lnb_ref, ys_hbm, o_ref,
                    ybuf, sem, *, tb, D, n):
    i = pl.program_id(0)
    rows = tb * TOP_K

    def gather(slot, d_ref):
        def body(g, c):
            t0 = pl.multiple_of(g * SUBLANES, SUBLANES)
            dst0 = pl.multiple_of(slot * rows + t0, SUBLANES)
            for u in range(SUBLANES):
                for k in range(TOP_K):
                    d = d_ref[0, 0, (t0 + u) * TOP_K + k]
                    pltpu.make_async_copy(ys_hbm.at[pl.ds(d, 1)], ybuf.at[pl.ds(dst0 + (k * tb + u), 1)],
                                          sem.at[slot]).start(priority=k % 2)
            return c
        lax.fori_loop(0, tb // SUBLANES, body, 0)

    @pl.when(i == 0)
    def _():
        gather(0, dcur_ref)

    @pl.when(i + 1 < n)
    def _():
        gather((i + 1) % 2, dnxt_ref)

    slot = i % 2
    base = slot * rows
    pltpu.make_async_copy(ys_hbm.at[pl.ds(0, rows)], ybuf.at[pl.ds(pl.multiple_of(base, rows), rows)],
                          sem.at[slot]).wait()
    w = w_ref[...]
    f = sh_ref[...].astype(F32)
    for k in range(TOP_K):
        f = f + w[:, k:k + 1] * ybuf[pl.ds(pl.multiple_of(base + k * tb, tb), tb), :]
    o_ref[...] = _layer_norm(DN_ALPHA * h_ref[...] + g2_ref[0] * f, lng_ref[...], lnb_ref[...])


def _combine_ln(h2, shared, wts, ys, dest, tok_off, mod3, row_fn, ln_g, ln_b, tb=128):
    M, D = h2.shape
    tb = _tile(M, tb)
    n = M // tb
    off = tok_off // tb
    dest3 = dest.reshape(-1, 1, tb * TOP_K)
    rowblk = lambda w, o: pl.BlockSpec((tb, w), lambda i: (i + o, 0))
    const = lambda shape: pl.BlockSpec(shape, lambda i: (0,) * len(shape))
    smem = lambda fn: pl.BlockSpec((1, 1, tb * TOP_K), fn, memory_space=pltpu.SMEM)
    return pl.pallas_call(
        functools.partial(_combine_kernel, tb=tb, D=D, n=n),
        grid=(n,),
        in_specs=[smem(lambda i: (i + off, 0, 0)), smem(lambda i: (jnp.minimum(i + 1, n - 1) + off, 0, 0)),
                  rowblk(D, 0), rowblk(D, off), rowblk(TOP_K, off),
                  pl.BlockSpec((1, 1, D), lambda i: (row_fn(i), 0, 5)), const((1, D)), const((1, D)),
                  pl.BlockSpec(memory_space=pl.ANY)],
        out_specs=rowblk(D, 0),
        out_shape=jax.ShapeDtypeStruct((M, D), F32),
        scratch_shapes=[pltpu.VMEM((2 * tb * TOP_K, D), F32), pltpu.SemaphoreType.DMA((2,))],
        compiler_params=pltpu.CompilerParams(dimension_semantics=("arbitrary",), vmem_limit_bytes=VMEM_LIMIT,
                                             disable_bounds_checks=True),
        name="combine_ln",
    )(dest3, dest3, h2, shared, wts, mod3, ln_g, ln_b, ys)


MOE_TM = 256


def kernel(x, c, ctx, c_ctx, w_mod, b_mod, w_in, na_rpb, gq_q_norm, gq_k_norm, ssm_lam_re, ssm_lam_im, ssm_log_dt, ssm_b_re, ssm_b_im, ssm_c_re, ssm_c_im, ssm_d, ssm_glu_w, ssm_glu_b, w_br_na, w_br_gq, w_br_ssm, w_out, ln1_g, ln1_b, router_w, router_bias, exp_w1, exp_w3, exp_w2, sh_w1, sh_w3, sh_w2, ln2_g, ln2_b):
    B, S, D = x.shape
    L = ctx.shape[1]
    assert B < 2 * SUBLANES and D % LANES == 0
    cos, sin = _rope_tables(S)
    na_mask = _na_mask(S // GRID_W)
    c_all = jnp.zeros((2 * SUBLANES, D), F32).at[:B].set(c).at[B].set(c_ctx)
    lat_row = lambda b: b
    ctx_row = lambda b: B
    h, hc = x, ctx
    for l in range(DEPTH):
        need_ctx = l < DEPTH - 1
        mod3 = _modvec(c_all, w_mod, l, b_mod[l])[:, None, :]
        vec = lambda p: p[l].reshape(1, -1).astype(F32)

        w_in_b = w_in[l].astype(BF16)
        u = _modulate(h, mod3, lat_row, 0, 1)
        uc = _modulate(hc, mod3, ctx_row, 0, 1)
        z2 = _matmul(u.reshape(B * S, D), w_in_b, BF16)
        zc2 = _matmul(uc.reshape(B * L, D), w_in_b if need_ctx else w_in_b[:, :CTX_IN_W], BF16)
        z3, zc3 = z2.reshape(B, S, -1), zc2.reshape(B, L, -1)

        a_out = _na_attention(z3, zc3, _na_bias_tiles(na_rpb[l]), na_mask)
        qn, kn = vec(gq_q_norm), vec(gq_k_norm)
        b_out = _gqa_attention(z3, zc3, cos, sin, qn, kn)

        wb, wc, a = _s5_weights(ssm_lam_re[l], ssm_lam_im[l], ssm_log_dt[l], ssm_b_re[l], ssm_b_im[l],
                                ssm_c_re[l], ssm_c_im[l])
        su = lambda t: t[:, :, OFF_SU:OFF_SU + SSM_WIDTH].astype(F32).transpose(1, 0, 2)
        h0 = jnp.zeros((4, B, SSM_LANES), F32)
        ycf, ycb, hfin = _s5_scan(su(zc3), wb, wc, a, h0)
        yf, yb, _ = _s5_scan(su(z3), wb, wc, a, hfin)
        glu_w, glu_b, dvec = ssm_glu_w[l].astype(BF16), vec(ssm_glu_b), vec(ssm_d)
        c_out = _glu(z3, yf, yb, dvec, glu_w, glu_b)

        wa, wg, ws, wo = (w[l].astype(BF16) for w in (w_br_na, w_br_gq, w_br_ssm, w_out))
        m = _merge(a_out.reshape(B * S, -1), b_out.reshape(B * S, -1), c_out.reshape(B * S, -1), z2, wa, wg, ws)
        rw = router_w[l].astype(F32).T
        rw_hi = rw.astype(BF16)
        rw_lo = (rw - rw_hi.astype(F32)).astype(BF16)
        rbias = router_bias[l].astype(F32).reshape(N_EXPERTS, 1)
        g1, b1 = vec(ln1_g), vec(ln1_b)
        tm_o = _tile(S, 256)
        h2, tokr, tokb, idx, wts, rank, counts = _outproj_ln_route(
            m, wo, h.reshape(B * S, D), mod3, lambda i: i // (S // tm_o), g1, b1, rw_hi, rw_lo, rbias,
            jnp.zeros((N_EXPERTS, 1), F32), tm=tm_o)
        if need_ctx:
            a_ctx = _ctx_attention(zc3, OFF_NA_Q, OFF_NA_K, OFF_NA_V, 1, False, qn, kn)
            b_ctx = _ctx_attention(zc3, OFF_GQ_Q, OFF_GQ_K, OFF_GQ_V, GQ_GROUP, True, qn, kn)
            c_ctx_out = _glu(zc3, ycf, ycb, dvec, glu_w, glu_b)
            mc = _merge(a_ctx.reshape(B * L, -1), b_ctx.reshape(B * L, -1), c_ctx_out.reshape(B * L, -1),
                        zc2, wa, wg, ws)
            tm_c = _tile(L, 256)
            hc2, tokr_c, tokb_c, idx_c, wts_c, rank_c, counts = _outproj_ln_route(
                mc, wo, hc.reshape(B * L, D), mod3, lambda i: B, g1, b1, rw_hi, rw_lo, rbias, counts, tm=tm_c)
            tokr = jnp.concatenate([tokr, tokr_c], axis=0)
            tokb = jnp.concatenate([tokb, tokb_c], axis=0)
            idx = jnp.concatenate([idx, idx_c], axis=1)
            wts = jnp.concatenate([wts, wts_c], axis=1)
            rank = jnp.concatenate([rank, rank_c], axis=1)

        w13 = jnp.concatenate([sh_w1[l], sh_w3[l]], axis=1).astype(BF16)
        shared = _shared_expert(tokb, w13, sh_w2[l].astype(BF16))
        wts = wts.T
        dest, blk_exp, n_used, pad_start, pad_cnt = _route_plan(idx.T, rank.T, counts, MOE_TM)
        xs = _dispatch(tokr, dest, pad_start, pad_cnt, n_used, blk_exp.shape[0], MOE_TM)
        ys = _routed_experts(xs, blk_exp, n_used, exp_w1, exp_w3, exp_w2, l, MOE_TM)
        g2, b2 = vec(ln2_g), vec(ln2_b)
        tb_l = _tile(S, 128)
        h = _combine_ln(h2, shared, wts, ys, dest, 0, mod3, lambda i: i // (S // tb_l), g2, b2,
                        tb=tb_l).reshape(B, S, D)
        if need_ctx:
            hc = _combine_ln(hc2, shared, wts, ys, dest, B * S, mod3, lambda i: B, g2, b2,
                             tb=_tile(L, 128)).reshape(B, L, D)
    return h
```
